```python
import jax, jax.numpy as jnp
from jax import lax
import numpy as np

D_MODEL = 1024
BATCH = 8
SEQ = 8192
DEPTH = 1
DEC_BATCH = 32
DEC_SEQ = 32
PAST_LEN = 4096

CHUNK = 64
Q_BLOCK = 128
SB_HEADS = 8
SB_HEAD_DIM = 64
SB_WIDTH = SB_HEADS * SB_HEAD_DIM
CONV_WIDTH = 256
CONV_K = 31
CONV_STATE = CONV_K - 1
MEM_HEADS = 4
MEM_HEAD_DIM = 64
MEM_WIDTH = MEM_HEADS * MEM_HEAD_DIM
N_MEM = 256
MIX_WIDTH = SB_WIDTH + CONV_WIDTH + MEM_WIDTH
IN_SIZES = [SB_WIDTH] * 4 + [CONV_WIDTH] * 3 + [MEM_WIDTH] * 2
IN_WIDTH = sum(IN_SIZES)
IN_SPLITS = [int(i) for i in np.cumsum(IN_SIZES)[:-1]]
DEEPNORM_ALPHA = (2 * DEPTH) ** 0.25
DEEPNORM_BETA = (8 * DEPTH) ** -0.25
LN_EPS = 1e-5

kernel_name = "stickbreak_conformer_hybrid_stream_step"


def layer_norm(x, g, b):
    xf = x.astype(jnp.float32)
    mu = jnp.mean(xf, axis=-1, keepdims=True)
    var = jnp.mean(jnp.square(xf - mu), axis=-1, keepdims=True)
    y = (xf - mu) * lax.rsqrt(var + LN_EPS)
    return (y * g.astype(jnp.float32) + b.astype(jnp.float32)).astype(x.dtype)


def sb_attend(q, k, v, q_pos, k_pos):
    z = jnp.einsum('bqhd,bkhd->bhqk', q, k,
                   preferred_element_type=jnp.float32) * (SB_HEAD_DIM ** -0.5)
    mask = k_pos[None, :] < q_pos[:, None]
    log_fail = jnp.where(mask, jax.nn.log_sigmoid(-z), 0.0)
    log_after = lax.cumsum(log_fail, axis=3, reverse=True) - log_fail
    w = jnp.where(mask, jnp.exp(jax.nn.log_sigmoid(z) + log_after), 0.0)
    return jnp.einsum('bhqk,bkhd->bqhd', w.astype(v.dtype), v)


def sb_prompt(q, k, v):
    B, T = q.shape[0], q.shape[1]
    nb = T // Q_BLOCK
    qb = q.reshape(B, nb, Q_BLOCK, SB_HEADS, SB_HEAD_DIM).transpose(1, 0, 2, 3, 4)
    k_pos = jnp.arange(T)
    q_pos = k_pos.reshape(nb, Q_BLOCK)
    out = lax.map(lambda a: sb_attend(a[0], k, v, a[1], k_pos), (qb, q_pos))
    return out.transpose(1, 0, 2, 3, 4).reshape(B, T, SB_HEADS, SB_HEAD_DIM)


def causal_dwconv(u_pad, w, b):
    y = lax.conv_general_dilated(u_pad, w[:, None, :], window_strides=(1,),
                                 padding='VALID',
                                 dimension_numbers=('NWC', 'WIO', 'NWC'),
                                 feature_group_count=CONV_WIDTH)
    return y + b


def mem_attend(q, mk, mv):
    s = jnp.einsum('bqhd,bmhd->bhqm', q, mk,
                   preferred_element_type=jnp.float32) * (MEM_HEAD_DIM ** -0.5)
    p = jax.nn.softmax(s, axis=-1)
    return jnp.einsum('bhqm,bmhd->bqhd', p.astype(mv.dtype), mv)


def mem_kv(mem, w_mem_kv):
    B = mem.shape[0]
    kv = mem @ w_mem_kv
    mk, mv = jnp.split(kv, [MEM_WIDTH], axis=-1)
    return (mk.reshape(B, N_MEM, MEM_HEADS, MEM_HEAD_DIM),
            mv.reshape(B, N_MEM, MEM_HEADS, MEM_HEAD_DIM))


def hybrid_layer(x, conv_past, mk, mv, attend, w_in, conv_w, conv_b,
                 cln_g, cln_b, w_out, ln_g, ln_b):
    B, T = x.shape[0], x.shape[1]
    h = x @ w_in
    q, k, v, g_sb, a_c, b_c, g_c, q_m, g_m = jnp.split(h, IN_SPLITS, axis=-1)
    q = q.reshape(B, T, SB_HEADS, SB_HEAD_DIM)
    k = k.reshape(B, T, SB_HEADS, SB_HEAD_DIM)
    v = v.reshape(B, T, SB_HEADS, SB_HEAD_DIM)
    o_sb = attend(q, k, v).reshape(B, T, SB_WIDTH) * jax.nn.silu(g_sb)
    u = a_c * jax.nn.sigmoid(b_c)
    u_pad = jnp.concatenate([conv_past, u], axis=1)
    c = causal_dwconv(u_pad, conv_w, conv_b)
    o_c = jax.nn.silu(layer_norm(c, cln_g, cln_b)) * jax.nn.silu(g_c)
    o_m = mem_attend(q_m.reshape(B, T, MEM_HEADS, MEM_HEAD_DIM), mk, mv)
    o_m = o_m.reshape(B, T, MEM_WIDTH) * jax.nn.silu(g_m)
    mix = jnp.concatenate([o_sb, o_c, o_m], axis=-1)
    y = layer_norm(DEEPNORM_ALPHA * x + mix @ w_out, ln_g, ln_b)
    return y, k, v, u_pad[:, -CONV_STATE:]


def setup_inputs(seed: int = 0) -> dict:
    key = jax.random.key(seed)
    ks = jax.random.split(key, 20)
    n = jax.random.normal
    f32 = jnp.float32
    return {
        "x_prompt": n(ks[0], (BATCH, SEQ, D_MODEL), f32),
        "x_sample": n(ks[1], (DEC_BATCH, DEC_SEQ, D_MODEL), f32),
        "cache_sb_k": n(ks[2], (DEPTH, DEC_BATCH, PAST_LEN, SB_HEADS, SB_HEAD_DIM), f32),
        "cache_sb_v": n(ks[3], (DEPTH, DEC_BATCH, PAST_LEN, SB_HEADS, SB_HEAD_DIM), f32),
        "cache_conv": 0.5 * n(ks[4], (DEPTH, DEC_BATCH, CONV_STATE, CONV_WIDTH), f32),
        "cache_mem_k": n(ks[5], (DEPTH, DEC_BATCH, N_MEM, MEM_HEADS, MEM_HEAD_DIM), f32),
        "cache_mem_v": n(ks[6], (DEPTH, DEC_BATCH, N_MEM, MEM_HEADS, MEM_HEAD_DIM), f32),
        "mem_prompt": n(ks[7], (BATCH, N_MEM, D_MODEL), f32),
        "w_in": n(ks[8], (DEPTH, D_MODEL, IN_WIDTH), f32) * D_MODEL ** -0.5,
        "w_mem_kv": n(ks[9], (DEPTH, D_MODEL, 2 * MEM_WIDTH), f32) * D_MODEL ** -0.5,
        "conv_w": n(ks[10], (DEPTH, CONV_K, CONV_WIDTH), f32) * CONV_K ** -0.5,
        "conv_b": 0.01 * n(ks[11], (DEPTH, CONV_WIDTH), f32),
        "conv_ln_g": 1.0 + 0.01 * n(ks[12], (DEPTH, CONV_WIDTH), f32),
        "conv_ln_b": 0.01 * n(ks[13], (DEPTH, CONV_WIDTH), f32),
        "w_out": n(ks[14], (DEPTH, MIX_WIDTH, D_MODEL), f32) * (MIX_WIDTH ** -0.5) * DEEPNORM_BETA,
        "ln_g": 1.0 + 0.01 * n(ks[15], (DEPTH, D_MODEL), f32),
        "ln_b": 0.01 * n(ks[16], (DEPTH, D_MODEL), f32),
    }


def reference(x_prompt, x_sample, cache_sb_k, cache_sb_v, cache_conv, cache_mem_k,
              cache_mem_v, mem_prompt, w_in, w_mem_kv, conv_w, conv_b, conv_ln_g,
              conv_ln_b, w_out, ln_g, ln_b):
    yp, ys = x_prompt, x_sample
    kp_l, vp_l, cp_l, mkp_l, mvp_l = [], [], [], [], []
    ks_l, vs_l, cs_l = [], [], []
    n_new = x_sample.shape[1]
    past = cache_sb_k.shape[2]
    q_pos_s = past + jnp.arange(n_new)
    k_pos_s = jnp.arange(past + n_new)
    for l in range(DEPTH):
        mk_p, mv_p = mem_kv(mem_prompt, w_mem_kv[l])
        zero_conv = jnp.zeros((yp.shape[0], CONV_STATE, CONV_WIDTH), yp.dtype)
        yp, k_p, v_p, c_p = hybrid_layer(
            yp, zero_conv, mk_p, mv_p, sb_prompt, w_in[l], conv_w[l], conv_b[l],
            conv_ln_g[l], conv_ln_b[l], w_out[l], ln_g[l], ln_b[l])
        kp_l.append(k_p); vp_l.append(v_p); cp_l.append(c_p)
        mkp_l.append(mk_p); mvp_l.append(mv_p)
        kc, vc = cache_sb_k[l], cache_sb_v[l]
        attend_s = lambda q, k, v, kc=kc, vc=vc: sb_attend(
            q, jnp.concatenate([kc, k], axis=1), jnp.concatenate([vc, v], axis=1),
            q_pos_s, k_pos_s)
        ys, k_s, v_s, c_s = hybrid_layer(
            ys, cache_conv[l], cache_mem_k[l], cache_mem_v[l], attend_s, w_in[l],
            conv_w[l], conv_b[l], conv_ln_g[l], conv_ln_b[l], w_out[l], ln_g[l], ln_b[l])
        ks_l.append(k_s); vs_l.append(v_s); cs_l.append(c_s)
    sb_k_prompt = jnp.stack(kp_l)
    sb_v_prompt = jnp.stack(vp_l)
    conv_prompt = jnp.stack(cp_l)
    mem_k_prompt = jnp.stack(mkp_l)
    mem_v_prompt = jnp.stack(mvp_l)
    sb_k_sample = jnp.stack(ks_l)
    sb_v_sample = jnp.stack(vs_l)
    conv_sample = jnp.stack(cs_l)
    return (yp, ys, sb_k_prompt, sb_v_prompt, conv_prompt, mem_k_prompt, mem_v_prompt,
            sb_k_sample, sb_v_sample, conv_sample)
```

```python
import functools

import jax
import jax.numpy as jnp
from jax import lax
from jax.experimental import pallas as pl
from jax.experimental.pallas import tpu as pltpu

F32 = jnp.float32
BF16 = jnp.bfloat16

D_MODEL = 1024
SB_HEADS = 8
HEAD_DIM = 64
SB_WIDTH = SB_HEADS * HEAD_DIM
CONV_WIDTH = 256
CONV_K = 31
CONV_STATE = CONV_K - 1
MEM_HEADS = 4
MEM_WIDTH = MEM_HEADS * HEAD_DIM
N_MEM = 256
IN_WIDTH = 4 * SB_WIDTH + 3 * CONV_WIDTH + 2 * MEM_WIDTH
COL_Q, COL_K, COL_V, COL_GA = 0, SB_WIDTH, 2 * SB_WIDTH, 3 * SB_WIDTH
COL_CA = 4 * SB_WIDTH
COL_CB = COL_CA + CONV_WIDTH
COL_GC = COL_CB + CONV_WIDTH
COL_QM = COL_GC + CONV_WIDTH
COL_GM = COL_QM + MEM_WIDTH
COL_MIX_A, COL_MIX_C, COL_MIX_M = 0, SB_WIDTH, SB_WIDTH + CONV_WIDTH
LN_EPS = 1e-5
QK_SCALE = HEAD_DIM ** -0.5

LANES = 128
PAIRS = SB_HEADS // 2
TQ = 128
TK = 128
HALO = 32
HALO_PAD = HALO - CONV_STATE
CONV_ROWS = 64
PROJ_COLS = 256
SKIP_LOG = -110.0
VMEM_LIMIT = 60 * 1024 * 1024

NT_DIMS = (((1,), (1,)), ((), ()))


def _sigmoid(x):
    return 1.0 / (1.0 + jnp.exp(-x))


def _silu(x):
    return x * _sigmoid(x)


def _project(xb, w_in_ref, h_ref):
    for c in range(IN_WIDTH // PROJ_COLS):
        cols = slice(c * PROJ_COLS, (c + 1) * PROJ_COLS)
        h_ref[:, cols] = jnp.dot(xb, w_in_ref[:, cols], preferred_element_type=F32)


def _lane_lo(n_rows):
    return lax.broadcasted_iota(jnp.int32, (n_rows, LANES), 1) < HEAD_DIM


def _build_qm(h_ref, rows, qm_ref):
    lane_lo = _lane_lo(qm_ref.shape[1])
    for p in range(PAIRS):
        qp = h_ref[rows, COL_Q + p * LANES:COL_Q + (p + 1) * LANES] * QK_SCALE
        qm_ref[2 * p] = jnp.where(lane_lo, qp, 0.0).astype(BF16)
        qm_ref[2 * p + 1] = jnp.where(lane_lo, 0.0, qp).astype(BF16)


def _sb_block(qm_ref, acc_ref, carry_ref, u2_ref, get_k, get_v, mask):
    for p in range(PAIRS):
        kp = get_k(p)
        vp = get_v(p)
        for h in (2 * p, 2 * p + 1):
            z = lax.dot_general(qm_ref[h], kp, NT_DIMS, preferred_element_type=F32)
            soft = jnp.log(1.0 + jnp.exp(-jnp.abs(z)))
            log_fail = jnp.minimum(-z, 0.0) - soft
            log_hit = log_fail + z
            if mask is not None:
                log_fail = jnp.where(mask, log_fail, 0.0)
            hi = log_fail.astype(BF16)
            lo = (log_fail - hi.astype(F32)).astype(BF16)
            sums = jnp.dot(jnp.concatenate([hi, lo], axis=1), u2_ref[...],
                           preferred_element_type=F32)
            carry = carry_ref[h]
            w = jnp.exp(log_hit + sums[:, :TK] + carry)
            if mask is not None:
                w = jnp.where(mask, w, 0.0)
            carry_ref[h] = carry + sums[:, TK:]
            acc_ref[h] += jnp.dot(w.astype(BF16), vp, preferred_element_type=F32)


def _carry_max(carry_ref):
    m = carry_ref[0]
    for h in range(1, SB_HEADS):
        m = jnp.maximum(m, carry_ref[h])
    return jnp.max(m)


def _sb_walk(first, n_blocks, m0, block_fn, carry_ref):
    def cond(c):
        j, m = c
        return jnp.logical_and(j < n_blocks, m > SKIP_LOG)

    def body(c):
        j, _ = c
        block_fn(j)
        return j + 1, _carry_max(carry_ref)

    return lax.while_loop(cond, body, (first, m0))


def _sb_finish(h_ref, rows, acc_ref, mix_ref):
    lane_lo = _lane_lo(acc_ref.shape[1])
    for p in range(PAIRS):
        o = jnp.where(lane_lo, acc_ref[2 * p], acc_ref[2 * p + 1])
        g = h_ref[rows, COL_GA + p * LANES:COL_GA + (p + 1) * LANES]
        mix_ref[rows, COL_MIX_A + p * LANES:COL_MIX_A + (p + 1) * LANES] = (
            o * _silu(g)).astype(BF16)


def _conv_module(h_ref, rows0, upad_ref, upad0, n_rows, mix_ref, cw_ref, cb_ref, g_ref, b_ref):
    for r in range(0, n_rows, CONV_ROWS):
        nr = min(CONV_ROWS, n_rows - r)
        c = jnp.broadcast_to(cb_ref[...], (nr, CONV_WIDTH))
        for i in range(CONV_K):
            c = c + cw_ref[i:i + 1, :] * upad_ref[pl.ds(upad0 + r + HALO_PAD + i, nr), :]
        mu = jnp.mean(c, axis=-1, keepdims=True)
        d = c - mu
        var = jnp.mean(d * d, axis=-1, keepdims=True)
        n = d * lax.rsqrt(var + LN_EPS) * g_ref[...] + b_ref[...]
        gate = h_ref[pl.ds(rows0 + r, nr), COL_GC:COL_GC + CONV_WIDTH]
        mix_ref[pl.ds(rows0 + r, nr), COL_MIX_C:COL_MIX_C + CONV_WIDTH] = (
            _silu(n) * _silu(gate)).astype(BF16)


def _mem_attend(h_ref, rows, n_rows, mk, mv, mix_ref):
    lane_lo = _lane_lo(n_rows)
    for p in range(MEM_HEADS // 2):
        cols = slice(p * LANES, (p + 1) * LANES)
        qp = h_ref[rows, COL_QM + p * LANES:COL_QM + (p + 1) * LANES] * QK_SCALE
        mkp = mk[:, cols]
        mvp = mv[:, cols]
        outs = []
        for hh in range(2):
            qh = (jnp.where(lane_lo, qp, 0.0) if hh == 0 else jnp.where(lane_lo, 0.0, qp)).astype(BF16)
            s = lax.dot_general(qh, mkp, NT_DIMS, preferred_element_type=F32)
            e = jnp.exp(s - jnp.max(s, axis=-1, keepdims=True))
            o = jnp.dot(e.astype(BF16), mvp, preferred_element_type=F32)
            outs.append(o / jnp.sum(e, axis=-1, keepdims=True))
        o = jnp.where(lane_lo, outs[0], outs[1])
        g = h_ref[rows, COL_GM + p * LANES:COL_GM + (p + 1) * LANES]
        mix_ref[rows, COL_MIX_M + p * LANES:COL_MIX_M + (p + 1) * LANES] = (o * _silu(g)).astype(BF16)


def _out_norm(x, mix_ref, w_out_ref, g_ref, b_ref, alpha):
    r = alpha * x + jnp.dot(mix_ref[...], w_out_ref[...], preferred_element_type=F32)
    mu = jnp.mean(r, axis=-1, keepdims=True)
    d = r - mu
    var = jnp.mean(d * d, axis=-1, keepdims=True)
    return d * lax.rsqrt(var + LN_EPS) * g_ref[...] + b_ref[...]


def _causal_mask(n_rows):
    row = lax.broadcasted_iota(jnp.int32, (n_rows, TK), 0)
    col = lax.broadcasted_iota(jnp.int32, (n_rows, TK), 1)
    return col < row


def _prompt_kernel(x_ref, mem_ref, w_in_ref, w_mem_ref, w_out_ref, u2_ref, cw_ref, cb_ref,
                   cg_ref, cbeta_ref, lg_ref, lb_ref,
                   y_ref, k_ref, v_ref, conv_ref, mk_ref, mv_ref,
                   h_scr, ks_scr, vs_scr, mk_scr, mv_scr, qm_scr, acc_scr, carry_scr,
                   upad_scr, mix_scr, *, tm, alpha):
    t = pl.program_id(1)
    causal = _causal_mask(TQ)

    @pl.when(t == 0)
    def _():
        kv = jnp.dot(mem_ref[0].astype(BF16), w_mem_ref[...], preferred_element_type=F32)
        mk_ref[0] = kv[:, :MEM_WIDTH]
        mv_ref[0] = kv[:, MEM_WIDTH:]
        mk_scr[...] = kv[:, :MEM_WIDTH].astype(BF16)
        mv_scr[...] = kv[:, MEM_WIDTH:].astype(BF16)
        upad_scr[0:HALO, :] = jnp.zeros((HALO, CONV_WIDTH), F32)

    _project(x_ref[0].astype(BF16), w_in_ref, h_scr)

    row0 = pl.multiple_of(t * tm, tm)
    k = h_scr[:, COL_K:COL_K + SB_WIDTH]
    v = h_scr[:, COL_V:COL_V + SB_WIDTH]
    k_ref[0] = k
    v_ref[0] = v
    ks_scr[pl.ds(row0, tm), :] = k.astype(BF16)
    vs_scr[pl.ds(row0, tm), :] = v.astype(BF16)

    for i in range(tm // TQ):
        rows = slice(i * TQ, (i + 1) * TQ)
        q0 = row0 + i * TQ
        _build_qm(h_scr, rows, qm_scr)
        acc_scr[...] = jnp.zeros(acc_scr.shape, F32)
        carry_scr[...] = jnp.zeros(carry_scr.shape, F32)

        def block(start, mask):
            start = pl.multiple_of(start, TK)
            _sb_block(qm_scr, acc_scr, carry_scr, u2_ref,
                      lambda p: ks_scr[pl.ds(start, TK), p * LANES:(p + 1) * LANES],
                      lambda p: vs_scr[pl.ds(start, TK), p * LANES:(p + 1) * LANES],
                      mask)

        block(q0, causal)
        n_prev = q0 // TK
        _sb_walk(jnp.int32(0), n_prev, jnp.float32(0.0),
                 lambda j: block((n_prev - 1 - j) * TK, None), carry_scr)
        _sb_finish(h_scr, rows, acc_scr, mix_scr)

    u = h_scr[:, COL_CA:COL_CA + CONV_WIDTH] * _sigmoid(h_scr[:, COL_CB:COL_CB + CONV_WIDTH])
    upad_scr[HALO:HALO + tm, :] = u
    _conv_module(h_scr, 0, upad_scr, 0, tm, mix_scr, cw_ref, cb_ref, cg_ref, cbeta_ref)
    tail = upad_scr[tm:tm + HALO, :]
    upad_scr[0:HALO, :] = tail
    conv_ref[0] = tail

    _mem_attend(h_scr, slice(0, tm), tm, mk_scr[...], mv_scr[...], mix_scr)
    y_ref[0] = _out_norm(x_ref[0], mix_scr, w_out_ref, lg_ref, lb_ref, alpha)


def _const_spec(shape):
    return pl.BlockSpec(shape, lambda *_: (0,) * len(shape), pipeline_mode=pl.Buffered(1))


def _prompt_call(x, mem, w_in, w_mem, w_out, u2, cw, cb, cg, cbeta, lg, lb, alpha, tm):
    B, T, _ = x.shape
    assert T % tm == 0 and tm % TQ == 0 and TQ == TK
    grid = (B, T // tm)
    row_spec = lambda width: pl.BlockSpec((1, tm, width), lambda b, t: (b, t, 0))
    per_b = lambda r, width: pl.BlockSpec((1, r, width), lambda b, t: (b, 0, 0))
    out_shape = (
        jax.ShapeDtypeStruct((B, T, D_MODEL), F32),
        jax.ShapeDtypeStruct((B, T, SB_WIDTH), F32),
        jax.ShapeDtypeStruct((B, T, SB_WIDTH), F32),
        jax.ShapeDtypeStruct((B, HALO, CONV_WIDTH), F32),
        jax.ShapeDtypeStruct((B, N_MEM, MEM_WIDTH), F32),
        jax.ShapeDtypeStruct((B, N_MEM, MEM_WIDTH), F32),
    )
    scratch = [
        pltpu.VMEM((tm, IN_WIDTH), F32),
        pltpu.VMEM((T, SB_WIDTH), BF16),
        pltpu.VMEM((T, SB_WIDTH), BF16),
        pltpu.VMEM((N_MEM, MEM_WIDTH), BF16),
        pltpu.VMEM((N_MEM, MEM_WIDTH), BF16),
        pltpu.VMEM((SB_HEADS, TQ, LANES), BF16),
        pltpu.VMEM((SB_HEADS, TQ, LANES), F32),
        pltpu.VMEM((SB_HEADS, TQ, LANES), F32),
        pltpu.VMEM((HALO + tm, CONV_WIDTH), F32),
        pltpu.VMEM((tm, D_MODEL), BF16),
    ]
    return pl.pallas_call(
        functools.partial(_prompt_kernel, tm=tm, alpha=alpha),
        grid=grid,
        in_specs=[
            row_spec(D_MODEL),
            per_b(N_MEM, D_MODEL),
            _const_spec((D_MODEL, IN_WIDTH)),
            _const_spec((D_MODEL, 2 * MEM_WIDTH)),
            _const_spec((D_MODEL, D_MODEL)),
            _const_spec((2 * TK, 2 * TK)),
            _const_spec((CONV_K, CONV_WIDTH)),
            _const_spec((1, CONV_WIDTH)),
            _const_spec((1, CONV_WIDTH)),
            _const_spec((1, CONV_WIDTH)),
            _const_spec((1, D_MODEL)),
            _const_spec((1, D_MODEL)),
        ],
        out_specs=(
            row_spec(D_MODEL), row_spec(SB_WIDTH), row_spec(SB_WIDTH),
            per_b(HALO, CONV_WIDTH), per_b(N_MEM, MEM_WIDTH), per_b(N_MEM, MEM_WIDTH),
        ),
        out_shape=out_shape,
        scratch_shapes=scratch,
        compiler_params=pltpu.CompilerParams(
            dimension_semantics=("arbitrary", "arbitrary"),
            vmem_limit_bytes=VMEM_LIMIT),
        name="prompt_layer",
    )(x, mem, w_in, w_mem, w_out, u2, cw, cb, cg, cbeta, lg, lb)


def _sample_kernel(x_ref, kwin_ref, vwin_ref, kc_hbm, vc_hbm, cpast_ref, mkc_ref, mvc_ref,
                   w_in_ref, w_out_ref, u2_ref, cw_ref, cb_ref, cg_ref, cbeta_ref, lg_ref, lb_ref,
                   y_ref, k_ref, v_ref, conv_ref,
                   h_scr, knew_scr, vnew_scr, kbuf, vbuf, sem, qm_scr, acc_scr, carry_scr,
                   upad_scr, mix_scr, *, group, n_new, win, past, alpha):
    step = pl.program_id(0)
    causal = _causal_mask(n_new)
    n_win = win // TK
    n_blocks = past // TK

    _project(x_ref[...].astype(BF16), w_in_ref, h_scr)
    k_ref[...] = h_scr[:, COL_K:COL_K + SB_WIDTH]
    v_ref[...] = h_scr[:, COL_V:COL_V + SB_WIDTH]

    knew_scr[...] = jnp.zeros(knew_scr.shape, BF16)
    vnew_scr[...] = jnp.zeros(vnew_scr.shape, BF16)

    def per_stream(g, _):
        r0 = pl.multiple_of(g * n_new, n_new)
        rows = pl.ds(r0, n_new)
        stream = step * group + g
        _build_qm(h_scr, rows, qm_scr)
        acc_scr[...] = jnp.zeros(acc_scr.shape, F32)
        carry_scr[...] = jnp.zeros(carry_scr.shape, F32)
        knew_scr[0:n_new, :] = h_scr[rows, COL_K:COL_K + SB_WIDTH].astype(BF16)
        vnew_scr[0:n_new, :] = h_scr[rows, COL_V:COL_V + SB_WIDTH].astype(BF16)

        _sb_block(qm_scr, acc_scr, carry_scr, u2_ref,
                  lambda p: knew_scr[:, p * LANES:(p + 1) * LANES],
                  lambda p: vnew_scr[:, p * LANES:(p + 1) * LANES], causal)

        def win_block(j):
            start = pl.multiple_of(win - (j + 1) * TK, TK)
            _sb_block(qm_scr, acc_scr, carry_scr, u2_ref,
                      lambda p: kwin_ref[g, pl.ds(start, TK), p * LANES:(p + 1) * LANES].astype(BF16),
                      lambda p: vwin_ref[g, pl.ds(start, TK), p * LANES:(p + 1) * LANES].astype(BF16),
                      None)

        def far_block(j):
            start = pl.multiple_of(past - (j + 1) * TK, TK)
            ck = pltpu.make_async_copy(kc_hbm.at[stream, pl.ds(start, TK), :], kbuf, sem.at[0])
            cv = pltpu.make_async_copy(vc_hbm.at[stream, pl.ds(start, TK), :], vbuf, sem.at[1])
            ck.start()
            cv.start()
            ck.wait()
            cv.wait()
            _sb_block(qm_scr, acc_scr, carry_scr, u2_ref,
                      lambda p: kbuf[:, p * LANES:(p + 1) * LANES].astype(BF16),
                      lambda p: vbuf[:, p * LANES:(p + 1) * LANES].astype(BF16), None)

        j, m = _sb_walk(jnp.int32(0), n_win, jnp.float32(0.0), win_block, carry_scr)
        _sb_walk(j, n_blocks, m, far_block, carry_scr)
        _sb_finish(h_scr, rows, acc_scr, mix_scr)
        _mem_attend(h_scr, rows, n_new, mkc_ref[g].astype(BF16), mvc_ref[g].astype(BF16), mix_scr)
        return 0

    lax.fori_loop(0, group, per_stream, 0)

    u = h_scr[:, COL_CA:COL_CA + CONV_WIDTH] * _sigmoid(h_scr[:, COL_CB:COL_CB + CONV_WIDTH])
    for g in range(group):
        base = g * (HALO + n_new)
        upad_scr[base:base + HALO, :] = cpast_ref[g]
        upad_scr[base + HALO:base + HALO + n_new, :] = u[g * n_new:(g + 1) * n_new]
        _conv_module(h_scr, g * n_new, upad_scr, base, n_new, mix_scr, cw_ref, cb_ref, cg_ref, cbeta_ref)
        conv_ref[g] = upad_scr[base + n_new:base + n_new + HALO, :]

    y_ref[...] = _out_norm(x_ref[...], mix_scr, w_out_ref, lg_ref, lb_ref, alpha)


def _sample_call(x, kc, vc, cpast, mkc, mvc, w_in, w_out, u2, cw, cb, cg, cbeta, lg, lb,
                 alpha, group, win):
    S, n_new, _ = x.shape
    past = kc.shape[1]
    assert S % group == 0 and past % win == 0 and win % TK == 0 and n_new <= TK
    assert n_new % 16 == 0 and n_new >= CONV_STATE
    rows = group * n_new
    x2 = x.reshape(S * n_new, D_MODEL)
    row_spec = lambda width: pl.BlockSpec((rows, width), lambda i: (i, 0))
    grp = lambda r, width: pl.BlockSpec((group, r, width), lambda i: (i, 0, 0))
    win_spec = pl.BlockSpec((group, win, SB_WIDTH), lambda i: (i, past // win - 1, 0))
    any_spec = pl.BlockSpec(memory_space=pl.ANY)
    out_shape = (
        jax.ShapeDtypeStruct((S * n_new, D_MODEL), F32),
        jax.ShapeDtypeStruct((S * n_new, SB_WIDTH), F32),
        jax.ShapeDtypeStruct((S * n_new, SB_WIDTH), F32),
        jax.ShapeDtypeStruct((S, HALO, CONV_WIDTH), F32),
    )
    scratch = [
        pltpu.VMEM((rows, IN_WIDTH), F32),
        pltpu.VMEM((TK, SB_WIDTH), BF16),
        pltpu.VMEM((TK, SB_WIDTH), BF16),
        pltpu.VMEM((TK, SB_WIDTH), F32),
        pltpu.VMEM((TK, SB_WIDTH), F32),
        pltpu.SemaphoreType.DMA((2,)),
        pltpu.VMEM((SB_HEADS, n_new, LANES), BF16),
        pltpu.VMEM((SB_HEADS, n_new, LANES), F32),
        pltpu.VMEM((SB_HEADS, n_new, LANES), F32),
        pltpu.VMEM((group * (HALO + n_new), CONV_WIDTH), F32),
        pltpu.VMEM((rows, D_MODEL), BF16),
    ]
    return pl.pallas_call(
        functools.partial(_sample_kernel, group=group, n_new=n_new, win=win, past=past, alpha=alpha),
        grid=(S // group,),
        in_specs=[
            row_spec(D_MODEL), win_spec, win_spec, any_spec, any_spec,
            grp(HALO, CONV_WIDTH), grp(N_MEM, MEM_WIDTH), grp(N_MEM, MEM_WIDTH),
            _const_spec((D_MODEL, IN_WIDTH)),
            _const_spec((D_MODEL, D_MODEL)),
            _const_spec((2 * TK, 2 * TK)),
            _const_spec((CONV_K, CONV_WIDTH)),
            _const_spec((1, CONV_WIDTH)),
            _const_spec((1, CONV_WIDTH)),
            _const_spec((1, CONV_WIDTH)),
            _const_spec((1, D_MODEL)),
            _const_spec((1, D_MODEL)),
        ],
        out_specs=(row_spec(D_MODEL), row_spec(SB_WIDTH), row_spec(SB_WIDTH), grp(HALO, CONV_WIDTH)),
        out_shape=out_shape,
        scratch_shapes=scratch,
        compiler_params=pltpu.CompilerParams(
            dimension_semantics=("arbitrary",),
            vmem_limit_bytes=VMEM_LIMIT),
        name="sample_layer",
    )(x2, kc, vc, kc, vc, cpast, mkc, mvc, w_in, w_out, u2, cw, cb, cg, cbeta, lg, lb)


def _prefix_matrix():
    j = lax.broadcasted_iota(jnp.int32, (TK, TK), 0)
    s = lax.broadcasted_iota(jnp.int32, (TK, TK), 1)
    half = jnp.concatenate([(j > s).astype(BF16), jnp.ones((TK, TK), BF16)], axis=1)
    return jnp.concatenate([half, half], axis=0)


def kernel(x_prompt, x_sample, cache_sb_k, cache_sb_v, cache_conv, cache_mem_k, cache_mem_v, mem_prompt, w_in, w_mem_kv, conv_w, conv_b, conv_ln_g, conv_ln_b, w_out, ln_g, ln_b):
    depth = w_in.shape[0]
    assert depth == 1, "single layer only"
    alpha = (2 * depth) ** 0.25
    B, T, _ = x_prompt.shape
    S, n_new, _ = x_sample.shape
    past = cache_sb_k.shape[2]

    u2 = _prefix_matrix()
    w_in_b = w_in[0].astype(BF16)
    w_mem_b = w_mem_kv[0].astype(BF16)
    w_out_b = w_out[0].astype(BF16)
    cw = conv_w[0]
    cb = conv_b[0].reshape(1, CONV_WIDTH)
    cg = conv_ln_g[0].reshape(1, CONV_WIDTH)
    cbeta = conv_ln_b[0].reshape(1, CONV_WIDTH)
    lg = ln_g[0].reshape(1, D_MODEL)
    lb = ln_b[0].reshape(1, D_MODEL)

    yp, kp, vp, cp, mkp, mvp = _prompt_call(
        x_prompt, mem_prompt, w_in_b, w_mem_b, w_out_b, u2, cw, cb, cg, cbeta, lg, lb,
        alpha, tm=256)

    cpast = jnp.pad(cache_conv[0], ((0, 0), (HALO_PAD, 0), (0, 0)))
    ys, ks, vs, cs = _sample_call(
        x_sample,
        cache_sb_k[0].reshape(S, past, SB_WIDTH), cache_sb_v[0].reshape(S, past, SB_WIDTH),
        cpast,
        cache_mem_k[0].reshape(S, N_MEM, MEM_WIDTH), cache_mem_v[0].reshape(S, N_MEM, MEM_WIDTH),
        w_in_b, w_out_b, u2, cw, cb, cg, cbeta, lg, lb, alpha, group=4, win=256)

    return (
        yp,
        ys.reshape(S, n_new, D_MODEL),
        kp.reshape(1, B, T, SB_HEADS, HEAD_DIM),
        vp.reshape(1, B, T, SB_HEADS, HEAD_DIM),
        cp[None, :, HALO_PAD:, :],
        mkp.reshape(1, B, N_MEM, MEM_HEADS, HEAD_DIM),
        mvp.reshape(1, B, N_MEM, MEM_HEADS, HEAD_DIM),
        ks.reshape(1, S, n_new, SB_HEADS, HEAD_DIM),
        vs.reshape(1, S, n_new, SB_HEADS, HEAD_DIM),
        cs[None, :, HALO_PAD:, :],
    )
```

```python
import functools

import jax
import jax.numpy as jnp
from jax import lax
from jax.experimental import pallas as pl
from jax.experimental.pallas import tpu as pltpu

F32 = jnp.float32
BF16 = jnp.bfloat16

D_MODEL = 1024
SB_HEADS = 8
HEAD_DIM = 64
SB_WIDTH = SB_HEADS * HEAD_DIM
CONV_WIDTH = 256
CONV_K = 31
CONV_STATE = CONV_K - 1
MEM_HEADS = 4
MEM_WIDTH = MEM_HEADS * HEAD_DIM
N_MEM = 256
IN_WIDTH = 4 * SB_WIDTH + 3 * CONV_WIDTH + 2 * MEM_WIDTH
REF_K, REF_V_END = SB_WIDTH, 3 * SB_WIDTH
COL_Q = 0
COL_GA = SB_WIDTH
COL_CA = COL_GA + SB_WIDTH
COL_CB = COL_CA + CONV_WIDTH
COL_GC = COL_CB + CONV_WIDTH
COL_QM = COL_GC + CONV_WIDTH
COL_GM = COL_QM + MEM_WIDTH
H_WIDTH = COL_GM + MEM_WIDTH
COL_K = H_WIDTH
COL_V = COL_K + SB_WIDTH
COL_MIX_A, COL_MIX_C, COL_MIX_M = 0, SB_WIDTH, SB_WIDTH + CONV_WIDTH
LN_EPS = 1e-5
QK_SCALE = HEAD_DIM ** -0.5

LANES = 128
SUBLANES = 8
PAIRS = SB_HEADS // 2
TQ = 128
TK = 128
HEAD_GROUP = 8
HALO = 32
HALO_PAD = HALO - CONV_STATE
CONV_ROWS = 64
PROJ_COLS = 256
SKIP_LOG = -110.0
VMEM_LIMIT = 60 * 1024 * 1024

NT_DIMS = (((1,), (1,)), ((), ()))


def _sigmoid(x):
    return 1.0 / (1.0 + jnp.exp(-x))


def _silu(x):
    return x * _sigmoid(x)


def _project(xb, w_in_ref, h_ref, width):
    for c in range(width // PROJ_COLS):
        cols = slice(c * PROJ_COLS, (c + 1) * PROJ_COLS)
        h_ref[:, cols] = jnp.dot(xb, w_in_ref[:, cols], preferred_element_type=F32)


def _lane_lo(n_rows):
    return lax.broadcasted_iota(jnp.int32, (n_rows, LANES), 1) < HEAD_DIM


def _build_qm(h_ref, rows, qm_ref):
    lane_lo = _lane_lo(qm_ref.shape[1])
    for p in range(PAIRS):
        qp = h_ref[rows, COL_Q + p * LANES:COL_Q + (p + 1) * LANES] * QK_SCALE
        qm_ref[2 * p] = jnp.where(lane_lo, qp, 0.0).astype(BF16)
        qm_ref[2 * p + 1] = jnp.where(lane_lo, 0.0, qp).astype(BF16)


def _sb_block(qm_ref, acc_ref, carry_ref, u2_ref, get_kt, get_vt, mask):
    for g0 in range(0, SB_HEADS, HEAD_GROUP):
        heads = range(g0, g0 + HEAD_GROUP)
        zs = [jnp.dot(qm_ref[h], get_kt(h // 2), preferred_element_type=F32) for h in heads]
        hits, sums = [], []
        for z in zs:
            soft = jnp.log(1.0 + jnp.exp(-jnp.abs(z)))
            log_fail = jnp.minimum(-z, 0.0) - soft
            hits.append(log_fail + z)
            if mask is not None:
                log_fail = jnp.where(mask, log_fail, 0.0)
            hi = log_fail.astype(BF16)
            lo = (log_fail - hi.astype(F32)).astype(BF16)
            sums.append(jnp.dot(jnp.concatenate([hi, lo], axis=1), u2_ref[...],
                                preferred_element_type=F32))
        for h, log_hit, s in zip(heads, hits, sums):
            carry = carry_ref[h]
            w = jnp.exp(log_hit + s[:, :TK] + carry)
            if mask is not None:
                w = jnp.where(mask, w, 0.0)
            carry_ref[h] = carry + s[:, TK:]
            acc_ref[h] += lax.dot_general(w.astype(BF16), get_vt(h // 2), NT_DIMS,
                                          preferred_element_type=F32)


def _carry_max(carry_ref):
    m = carry_ref[0]
    for h in range(1, SB_HEADS):
        m = jnp.maximum(m, carry_ref[h])
    return jnp.max(m)


def _sb_walk(first, n_blocks, m0, block_fn, carry_ref):
    def cond(c):
        j, m = c
        return jnp.logical_and(j < n_blocks, m > SKIP_LOG)

    def body(c):
        j, _ = c
        block_fn(j)
        return j + 1, _carry_max(carry_ref)

    return lax.while_loop(cond, body, (first, m0))


def _sb_finish(h_ref, rows, acc_ref, mix_ref):
    lane_lo = _lane_lo(acc_ref.shape[1])
    for p in range(PAIRS):
        o = jnp.where(lane_lo, acc_ref[2 * p], acc_ref[2 * p + 1])
        g = h_ref[rows, COL_GA + p * LANES:COL_GA + (p + 1) * LANES]
        mix_ref[rows, COL_MIX_A + p * LANES:COL_MIX_A + (p + 1) * LANES] = (
            o * _silu(g)).astype(BF16)


def _conv_module(h_ref, rows0, upad_ref, upad0, n_rows, mix_ref, cw_ref, cb_ref, g_ref, b_ref):
    for r in range(0, n_rows, CONV_ROWS):
        nr = min(CONV_ROWS, n_rows - r)
        c = jnp.broadcast_to(cb_ref[...], (nr, CONV_WIDTH))
        for s in range(SUBLANES):
            taps = [i for i in range(CONV_K) if (HALO_PAD + i) % SUBLANES == s]
            span = nr if s == 0 else nr + SUBLANES
            part = None
            for i in taps:
                base = upad0 + r + HALO_PAD + i - s
                term = cw_ref[i:i + 1, :] * upad_ref[pl.ds(base, span), :]
                part = term if part is None else part + term
            c = c + part[s:s + nr]
        mu = jnp.mean(c, axis=-1, keepdims=True)
        d = c - mu
        var = jnp.mean(d * d, axis=-1, keepdims=True)
        n = d * lax.rsqrt(var + LN_EPS) * g_ref[...] + b_ref[...]
        gate = h_ref[pl.ds(rows0 + r, nr), COL_GC:COL_GC + CONV_WIDTH]
        mix_ref[pl.ds(rows0 + r, nr), COL_MIX_C:COL_MIX_C + CONV_WIDTH] = (
            _silu(n) * _silu(gate)).astype(BF16)


def _mem_attend(h_ref, rows, n_rows, mkt, mvt, mix_ref):
    lane_lo = _lane_lo(n_rows)
    for p in range(MEM_HEADS // 2):
        qp = h_ref[rows, COL_QM + p * LANES:COL_QM + (p + 1) * LANES] * QK_SCALE
        mkp = mkt[p * LANES:(p + 1) * LANES, :]
        mvp = mvt[p * LANES:(p + 1) * LANES, :]
        outs = []
        for hh in range(2):
            qh = (jnp.where(lane_lo, qp, 0.0) if hh == 0 else jnp.where(lane_lo, 0.0, qp)).astype(BF16)
            s = jnp.dot(qh, mkp, preferred_element_type=F32)
            e = jnp.exp(s - jnp.max(s, axis=-1, keepdims=True))
            o = lax.dot_general(e.astype(BF16), mvp, NT_DIMS, preferred_element_type=F32)
            outs.append(o / jnp.sum(e, axis=-1, keepdims=True))
        o = jnp.where(lane_lo, outs[0], outs[1])
        g = h_ref[rows, COL_GM + p * LANES:COL_GM + (p + 1) * LANES]
        mix_ref[rows, COL_MIX_M + p * LANES:COL_MIX_M + (p + 1) * LANES] = (o * _silu(g)).astype(BF16)


def _out_norm(x, mix_ref, w_out_ref, g_ref, b_ref, alpha):
    r = alpha * x + jnp.dot(mix_ref[...], w_out_ref[...], preferred_element_type=F32)
    mu = jnp.mean(r, axis=-1, keepdims=True)
    d = r - mu
    var = jnp.mean(d * d, axis=-1, keepdims=True)
    return d * lax.rsqrt(var + LN_EPS) * g_ref[...] + b_ref[...]


def _prompt_kernel(x_ref, mem_ref, w_in_ref, w_kvt_ref, w_memt_ref, w_out_ref, u2_ref, cw_ref, cb_ref,
                   cg_ref, cbeta_ref, lg_ref, lb_ref,
                   y_ref, kt_ref, vt_ref, conv_ref, mkt_ref, mvt_ref,
                   h_scr, kt_scr, vt_scr, mkt_scr, mvt_scr, qm_scr, acc_scr, carry_scr,
                   upad_scr, mix_scr, *, tm, alpha):
    t = pl.program_id(1)
    row = lax.broadcasted_iota(jnp.int32, (TQ, TK), 0)
    col = lax.broadcasted_iota(jnp.int32, (TQ, TK), 1)
    causal = col < row

    @pl.when(t == 0)
    def _():
        kvt = lax.dot_general(w_memt_ref[...], mem_ref[0].astype(BF16), NT_DIMS,
                              preferred_element_type=F32)
        mkt_ref[0] = kvt[:MEM_WIDTH]
        mvt_ref[0] = kvt[MEM_WIDTH:]
        mkt_scr[...] = kvt[:MEM_WIDTH].astype(BF16)
        mvt_scr[...] = kvt[MEM_WIDTH:].astype(BF16)
        upad_scr[0:HALO, :] = jnp.zeros((HALO, CONV_WIDTH), F32)

    xb = x_ref[0].astype(BF16)
    _project(xb, w_in_ref, h_scr, H_WIDTH)
    kvt = lax.dot_general(w_kvt_ref[...], xb, NT_DIMS, preferred_element_type=F32)
    kt_ref[0] = kvt[:SB_WIDTH]
    vt_ref[0] = kvt[SB_WIDTH:]
    blk0 = t * (tm // TK)
    for c in range(tm // TK):
        kt_scr[blk0 + c] = kvt[:SB_WIDTH, c * TK:(c + 1) * TK].astype(BF16)
        vt_scr[blk0 + c] = kvt[SB_WIDTH:, c * TK:(c + 1) * TK].astype(BF16)

    for i in range(tm // TQ):
        rows = slice(i * TQ, (i + 1) * TQ)
        qblk = blk0 + i
        _build_qm(h_scr, rows, qm_scr)
        acc_scr[...] = jnp.zeros(acc_scr.shape, F32)
        carry_scr[...] = jnp.zeros(carry_scr.shape, F32)

        def block(blk, mask):
            _sb_block(qm_scr, acc_scr, carry_scr, u2_ref,
                      lambda p: kt_scr[blk, p * LANES:(p + 1) * LANES, :],
                      lambda p: vt_scr[blk, p * LANES:(p + 1) * LANES, :],
                      mask)

        block(qblk, causal)
        _sb_walk(jnp.int32(0), qblk, jnp.float32(0.0),
                 lambda j: block(qblk - 1 - j, None), carry_scr)
        _sb_finish(h_scr, rows, acc_scr, mix_scr)

    u = h_scr[:, COL_CA:COL_CA + CONV_WIDTH] * _sigmoid(h_scr[:, COL_CB:COL_CB + CONV_WIDTH])
    upad_scr[HALO:HALO + tm, :] = u
    _conv_module(h_scr, 0, upad_scr, 0, tm, mix_scr, cw_ref, cb_ref, cg_ref, cbeta_ref)
    tail = upad_scr[tm:tm + HALO, :]
    upad_scr[0:HALO, :] = tail
    conv_ref[0] = tail

    _mem_attend(h_scr, slice(0, tm), tm, mkt_scr[...], mvt_scr[...], mix_scr)
    y_ref[0] = _out_norm(x_ref[0], mix_scr, w_out_ref, lg_ref, lb_ref, alpha)


def _const_spec(shape):
    return pl.BlockSpec(shape, lambda *_: (0,) * len(shape), pipeline_mode=pl.Buffered(1))


def _prompt_call(x, mem, w_in, w_kvt, w_memt, w_out, u2, cw, cb, cg, cbeta, lg, lb, alpha, tm):
    B, T, _ = x.shape
    assert T % tm == 0 and tm % TQ == 0 and TQ == TK
    grid = (B, T // tm)
    row_spec = lambda width: pl.BlockSpec((1, tm, width), lambda b, t: (b, t, 0))
    col_spec = lambda height: pl.BlockSpec((1, height, tm), lambda b, t: (b, 0, t))
    per_b = lambda r, width: pl.BlockSpec((1, r, width), lambda b, t: (b, 0, 0))
    out_shape = (
        jax.ShapeDtypeStruct((B, T, D_MODEL), F32),
        jax.ShapeDtypeStruct((B, SB_WIDTH, T), F32),
        jax.ShapeDtypeStruct((B, SB_WIDTH, T), F32),
        jax.ShapeDtypeStruct((B, HALO, CONV_WIDTH), F32),
        jax.ShapeDtypeStruct((B, MEM_WIDTH, N_MEM), F32),
        jax.ShapeDtypeStruct((B, MEM_WIDTH, N_MEM), F32),
    )
    scratch = [
        pltpu.VMEM((tm, H_WIDTH), F32),
        pltpu.VMEM((T // TK, SB_WIDTH, TK), BF16),
        pltpu.VMEM((T // TK, SB_WIDTH, TK), BF16),
        pltpu.VMEM((MEM_WIDTH, N_MEM), BF16),
        pltpu.VMEM((MEM_WIDTH, N_MEM), BF16),
        pltpu.VMEM((SB_HEADS, TQ, LANES), BF16),
        pltpu.VMEM((SB_HEADS, TQ, LANES), F32),
        pltpu.VMEM((SB_HEADS, TQ, LANES), F32),
        pltpu.VMEM((HALO + tm, CONV_WIDTH), F32),
        pltpu.VMEM((tm, D_MODEL), BF16),
    ]
    return pl.pallas_call(
        functools.partial(_prompt_kernel, tm=tm, alpha=alpha),
        grid=grid,
        in_specs=[
            row_spec(D_MODEL),
            per_b(N_MEM, D_MODEL),
            _const_spec((D_MODEL, H_WIDTH)),
            _const_spec((2 * SB_WIDTH, D_MODEL)),
            _const_spec((2 * MEM_WIDTH, D_MODEL)),
            _const_spec((D_MODEL, D_MODEL)),
            _const_spec((2 * TK, 2 * TK)),
            _const_spec((CONV_K, CONV_WIDTH)),
            _const_spec((1, CONV_WIDTH)),
            _const_spec((1, CONV_WIDTH)),
            _const_spec((1, CONV_WIDTH)),
            _const_spec((1, D_MODEL)),
            _const_spec((1, D_MODEL)),
        ],
        out_specs=(
            row_spec(D_MODEL), col_spec(SB_WIDTH), col_spec(SB_WIDTH),
            per_b(HALO, CONV_WIDTH), per_b(MEM_WIDTH, N_MEM), per_b(MEM_WIDTH, N_MEM),
        ),
        out_shape=out_shape,
        scratch_shapes=scratch,
        compiler_params=pltpu.CompilerParams(
            dimension_semantics=("arbitrary", "arbitrary"),
            vmem_limit_bytes=VMEM_LIMIT),
        name="prompt_layer",
    )(x, mem, w_in, w_kvt, w_memt, w_out, u2, cw, cb, cg, cbeta, lg, lb)


def _sample_kernel(x_ref, ktwin_ref, vtwin_ref, ktc_hbm, vtc_hbm, cpast_ref, mktc_ref, mvtc_ref,
                   w_in_ref, w_kvt_ref, w_out_ref, u2_ref, cw_ref, cb_ref, cg_ref, cbeta_ref,
                   lg_ref, lb_ref,
                   y_ref, k_ref, v_ref, conv_ref,
                   h_scr, ktnew_scr, vtnew_scr, ktbuf, vtbuf, sem, qm_scr, acc_scr, carry_scr,
                   upad_scr, mix_scr, *, group, n_new, win, past, alpha):
    step = pl.program_id(0)
    row = lax.broadcasted_iota(jnp.int32, (n_new, TK), 0)
    col = lax.broadcasted_iota(jnp.int32, (n_new, TK), 1)
    n_win = win // TK
    n_blocks = past // TK

    xb = x_ref[...].astype(BF16)
    _project(xb, w_in_ref, h_scr, IN_WIDTH)
    k_ref[...] = h_scr[:, COL_K:COL_K + SB_WIDTH]
    v_ref[...] = h_scr[:, COL_V:COL_V + SB_WIDTH]
    kvt = lax.dot_general(w_kvt_ref[...], xb, NT_DIMS, preferred_element_type=F32)
    ktnew_scr[...] = kvt[:SB_WIDTH].astype(BF16)
    vtnew_scr[...] = kvt[SB_WIDTH:].astype(BF16)

    def per_stream(g, _):
        r0 = pl.multiple_of(g * n_new, n_new)
        rows = pl.ds(r0, n_new)
        stream = step * group + g
        _build_qm(h_scr, rows, qm_scr)
        acc_scr[...] = jnp.zeros(acc_scr.shape, F32)
        carry_scr[...] = jnp.zeros(carry_scr.shape, F32)

        rel = col - r0
        own_past = jnp.logical_and(rel >= 0, rel < row)
        _sb_block(qm_scr, acc_scr, carry_scr, u2_ref,
                  lambda p: ktnew_scr[p * LANES:(p + 1) * LANES, :],
                  lambda p: vtnew_scr[p * LANES:(p + 1) * LANES, :], own_past)

        def win_block(j):
            cols = slice(win - (j + 1) * TK, win - j * TK)
            _sb_block(qm_scr, acc_scr, carry_scr, u2_ref,
                      lambda p: ktwin_ref[g, p * LANES:(p + 1) * LANES, cols].astype(BF16),
                      lambda p: vtwin_ref[g, p * LANES:(p + 1) * LANES, cols].astype(BF16),
                      None)
            return _carry_max(carry_scr)

        def far_block(j):
            start = pl.multiple_of(past - (j + 1) * TK, TK)
            ck = pltpu.make_async_copy(ktc_hbm.at[stream, :, pl.ds(start, TK)], ktbuf, sem.at[0])
            cv = pltpu.make_async_copy(vtc_hbm.at[stream, :, pl.ds(start, TK)], vtbuf, sem.at[1])
            ck.start()
            cv.start()
            ck.wait()
            cv.wait()
            _sb_block(qm_scr, acc_scr, carry_scr, u2_ref,
                      lambda p: ktbuf[p * LANES:(p + 1) * LANES, :].astype(BF16),
                      lambda p: vtbuf[p * LANES:(p + 1) * LANES, :].astype(BF16), None)

        m = jnp.float32(0.0)
        for j in range(n_win):
            m = lax.cond(m > SKIP_LOG, functools.partial(win_block, j), lambda m=m: m)
        _sb_walk(jnp.int32(n_win), n_blocks, m, far_block, carry_scr)
        _sb_finish(h_scr, rows, acc_scr, mix_scr)
        _mem_attend(h_scr, rows, n_new, mktc_ref[g].astype(BF16), mvtc_ref[g].astype(BF16), mix_scr)
        return 0

    lax.fori_loop(0, group, per_stream, 0)

    u = h_scr[:, COL_CA:COL_CA + CONV_WIDTH] * _sigmoid(h_scr[:, COL_CB:COL_CB + CONV_WIDTH])
    for g in range(group):
        base = g * (HALO + n_new)
        upad_scr[base:base + HALO, :] = cpast_ref[g]
        upad_scr[base + HALO:base + HALO + n_new, :] = u[g * n_new:(g + 1) * n_new]
        _conv_module(h_scr, g * n_new, upad_scr, base, n_new, mix_scr, cw_ref, cb_ref, cg_ref, cbeta_ref)
        conv_ref[g] = upad_scr[base + n_new:base + n_new + HALO, :]

    y_ref[...] = _out_norm(x_ref[...], mix_scr, w_out_ref, lg_ref, lb_ref, alpha)


def _sample_call(x, ktc, vtc, cpast, mktc, mvtc, w_in, w_kvt, w_out, u2, cw, cb, cg, cbeta, lg, lb,
                 alpha, win):
    S, n_new, _ = x.shape
    past = ktc.shape[2]
    group = TK // n_new
    assert group * n_new == TK and S % group == 0 and past % win == 0 and win % TK == 0
    assert n_new % 16 == 0 and n_new >= CONV_STATE
    rows = group * n_new
    x2 = x.reshape(S * n_new, D_MODEL)
    row_spec = lambda width: pl.BlockSpec((rows, width), lambda i: (i, 0))
    grp = lambda r, width: pl.BlockSpec((group, r, width), lambda i: (i, 0, 0))
    win_spec = pl.BlockSpec((group, SB_WIDTH, win), lambda i: (i, 0, past // win - 1))
    any_spec = pl.BlockSpec(memory_space=pl.ANY)
    out_shape = (
        jax.ShapeDtypeStruct((S * n_new, D_MODEL), F32),
        jax.ShapeDtypeStruct((S * n_new, SB_WIDTH), F32),
        jax.ShapeDtypeStruct((S * n_new, SB_WIDTH), F32),
        jax.ShapeDtypeStruct((S, HALO, CONV_WIDTH), F32),
    )
    scratch = [
        pltpu.VMEM((rows, IN_WIDTH), F32),
        pltpu.VMEM((SB_WIDTH, TK), BF16),
        pltpu.VMEM((SB_WIDTH, TK), BF16),
        pltpu.VMEM((SB_WIDTH, TK), F32),
        pltpu.VMEM((SB_WIDTH, TK), F32),
        pltpu.SemaphoreType.DMA((2,)),
        pltpu.VMEM((SB_HEADS, n_new, LANES), BF16),
        pltpu.VMEM((SB_HEADS, n_new, LANES), F32),
        pltpu.VMEM((SB_HEADS, n_new, LANES), F32),
        pltpu.VMEM((group * (HALO + n_new), CONV_WIDTH), F32),
        pltpu.VMEM((rows, D_MODEL), BF16),
    ]
    return pl.pallas_call(
        functools.partial(_sample_kernel, group=group, n_new=n_new, win=win, past=past, alpha=alpha),
        grid=(S // group,),
        in_specs=[
            row_spec(D_MODEL), win_spec, win_spec, any_spec, any_spec,
            grp(HALO, CONV_WIDTH), grp(MEM_WIDTH, N_MEM), grp(MEM_WIDTH, N_MEM),
            _const_spec((D_MODEL, IN_WIDTH)),
            _const_spec((2 * SB_WIDTH, D_MODEL)),
            _const_spec((D_MODEL, D_MODEL)),
            _const_spec((2 * TK, 2 * TK)),
            _const_spec((CONV_K, CONV_WIDTH)),
            _const_spec((1, CONV_WIDTH)),
            _const_spec((1, CONV_WIDTH)),
            _const_spec((1, CONV_WIDTH)),
            _const_spec((1, D_MODEL)),
            _const_spec((1, D_MODEL)),
        ],
        out_specs=(row_spec(D_MODEL), row_spec(SB_WIDTH), row_spec(SB_WIDTH), grp(HALO, CONV_WIDTH)),
        out_shape=out_shape,
        scratch_shapes=scratch,
        compiler_params=pltpu.CompilerParams(
            dimension_semantics=("arbitrary",),
            vmem_limit_bytes=VMEM_LIMIT),
        name="sample_layer",
    )(x2, ktc, vtc, ktc, vtc, cpast, mktc, mvtc, w_in, w_kvt, w_out, u2, cw, cb, cg, cbeta, lg, lb)


def _prefix_matrix():
    j = lax.broadcasted_iota(jnp.int32, (TK, TK), 0)
    s = lax.broadcasted_iota(jnp.int32, (TK, TK), 1)
    half = jnp.concatenate([(j > s).astype(BF16), jnp.ones((TK, TK), BF16)], axis=1)
    return jnp.concatenate([half, half], axis=0)


def _time_minor(a):
    n, time, heads, dim = a.shape
    return jnp.transpose(a, (0, 2, 3, 1)).reshape(n, heads * dim, time)


def _time_major(a, heads):
    n, width, time = a.shape
    return jnp.transpose(a.reshape(n, heads, width // heads, time), (0, 3, 1, 2))[None]


def kernel(x_prompt, x_sample, cache_sb_k, cache_sb_v, cache_conv, cache_mem_k, cache_mem_v, mem_prompt, w_in, w_mem_kv, conv_w, conv_b, conv_ln_g, conv_ln_b, w_out, ln_g, ln_b):
    depth = w_in.shape[0]
    assert depth == 1, "single layer only"
    alpha = (2 * depth) ** 0.25
    S, n_new, _ = x_sample.shape

    u2 = _prefix_matrix()
    w = w_in[0]
    w_kv = w[:, REF_K:REF_V_END]
    w_in_b = jnp.concatenate([w[:, :REF_K], w[:, REF_V_END:], w_kv], axis=1).astype(BF16)
    w_kvt = w_kv.T.astype(BF16)
    w_memt = w_mem_kv[0].T.astype(BF16)
    w_out_b = w_out[0].astype(BF16)
    cw = conv_w[0]
    cb = conv_b[0].reshape(1, CONV_WIDTH)
    cg = conv_ln_g[0].reshape(1, CONV_WIDTH)
    cbeta = conv_ln_b[0].reshape(1, CONV_WIDTH)
    lg = ln_g[0].reshape(1, D_MODEL)
    lb = ln_b[0].reshape(1, D_MODEL)

    yp, ktp, vtp, cp, mktp, mvtp = _prompt_call(
        x_prompt, mem_prompt, w_in_b, w_kvt, w_memt, w_out_b, u2, cw, cb, cg, cbeta, lg, lb,
        alpha, tm=256)

    cpast = jnp.pad(cache_conv[0], ((0, 0), (HALO_PAD, 0), (0, 0)))
    ys, ks, vs, cs = _sample_call(
        x_sample, _time_minor(cache_sb_k[0]), _time_minor(cache_sb_v[0]), cpast,
        _time_minor(cache_mem_k[0]), _time_minor(cache_mem_v[0]),
        w_in_b, w_kvt, w_out_b, u2, cw, cb, cg, cbeta, lg, lb, alpha, win=256)

    return (
        yp,
        ys.reshape(S, n_new, D_MODEL),
        _time_major(ktp, SB_HEADS),
        _time_major(vtp, SB_HEADS),
        cp[None, :, HALO_PAD:, :],
        _time_major(mktp, MEM_HEADS),
        _time_major(mvtp, MEM_HEADS),
        ks.reshape(1, S, n_new, SB_HEADS, HEAD_DIM),
        vs.reshape(1, S, n_new, SB_HEADS, HEAD_DIM),
        cs[None, :, HALO_PAD:, :],
    )
```

```python
import functools
import math
from typing import Any, Callable, NamedTuple

import jax
import jax.numpy as jnp
from jax import lax
from jax.experimental import pallas as pl
from jax.experimental.pallas import tpu as pltpu

F32 = jnp.float32
BF16 = jnp.bfloat16

D_MODEL = 1024
SB_HEADS = 8
HEAD_DIM = 64
SB_WIDTH = SB_HEADS * HEAD_DIM
CONV_WIDTH = 256
CONV_K = 31
CONV_STATE = CONV_K - 1
MEM_HEADS = 4
MEM_WIDTH = MEM_HEADS * HEAD_DIM
N_MEM = 256
IN_WIDTH = 4 * SB_WIDTH + 3 * CONV_WIDTH + 2 * MEM_WIDTH
REF_K, REF_V_END = SB_WIDTH, 3 * SB_WIDTH
COL_Q = 0
COL_GA = SB_WIDTH
COL_CA = COL_GA + SB_WIDTH
COL_CB = COL_CA + CONV_WIDTH
COL_GC = COL_CB + CONV_WIDTH
COL_QM = COL_GC + CONV_WIDTH
COL_GM = COL_QM + MEM_WIDTH
H_WIDTH = COL_GM + MEM_WIDTH
COL_K = H_WIDTH
COL_V = COL_K + SB_WIDTH
COL_MIX_A, COL_MIX_C, COL_MIX_M = 0, SB_WIDTH, SB_WIDTH + CONV_WIDTH
LN_EPS = 1e-5
QK_SCALE = HEAD_DIM ** -0.5

LANES = 128
SUBLANES = 8
PAIRS = SB_HEADS // 2
TQ = 128
TK = 128
HALO = 32
HALO_PAD = HALO - CONV_STATE
CONV_ROWS = 64
PROJ_COLS = 256
PROJ_SPLIT = 1024
INV_LN2 = 1.0 / math.log(2.0)
QK_SCALE_LOG2 = QK_SCALE * INV_LN2
SKIP_LOG2 = -160.0
VMEM_LIMIT = 60 * 1024 * 1024

NT_DIMS = (((1,), (1,)), ((), ()))


def _sigmoid(x):
    return 1.0 / (1.0 + jnp.exp(-x))


def _silu(x):
    return x * _sigmoid(x)


def _project(xb, w_in_ref, h_ref, first_col, last_col):
    for c in range(first_col // PROJ_COLS, last_col // PROJ_COLS):
        cols = slice(c * PROJ_COLS, (c + 1) * PROJ_COLS)
        h_ref[:, cols] = jnp.dot(xb, w_in_ref[:, cols], preferred_element_type=F32)


def _lane_lo(n_rows):
    return lax.broadcasted_iota(jnp.int32, (n_rows, LANES), 1) < HEAD_DIM


def _build_qm(h_ref, rows, qm_ref):
    n = qm_ref.shape[1] // 2
    lane_lo = _lane_lo(n)
    for p in range(PAIRS):
        qp = h_ref[rows, COL_Q + p * LANES:COL_Q + (p + 1) * LANES] * QK_SCALE_LOG2
        qm_ref[p, 0:n] = jnp.where(lane_lo, qp, 0.0).astype(BF16)
        qm_ref[p, n:2 * n] = jnp.where(lane_lo, 0.0, qp).astype(BF16)


class _Part(NamedTuple):
    qm: Any
    acc: Any
    carry: Any
    get_kt: Callable[[int], Any]
    get_vt: Callable[[int], Any]
    mask: Any


def _sb_blocks(parts, u2_ref):
    chains = [(part, p) for part in parts for p in range(PAIRS)]
    n2 = parts[0].qm.shape[1]
    zs = [jnp.dot(part.qm[p], part.get_kt(p), preferred_element_type=F32) for part, p in chains]
    hits, split = [], []
    for (part, _), z in zip(chains, zs):
        nz = -z
        soft = jnp.log(1.0 + jnp.exp2(jnp.minimum(z, nz))) * INV_LN2
        log_fail = jnp.minimum(nz, 0.0) - soft
        hits.append(log_fail + z)
        if part.mask is not None:
            log_fail = jnp.where(part.mask, log_fail, 0.0)
        hi = log_fail.astype(BF16)
        lo = (log_fail - hi.astype(F32)).astype(BF16)
        split.append(jnp.concatenate([hi, lo], axis=1))
    sums = jnp.dot(jnp.concatenate(split, axis=0), u2_ref[...],
                   preferred_element_type=F32)
    for i, ((part, p), log_hit) in enumerate(zip(chains, hits)):
        s = sums[i * n2:(i + 1) * n2]
        carry = part.carry[p]
        w = jnp.exp2(log_hit + s[:, :TK] + carry)
        if part.mask is not None:
            w = jnp.where(part.mask, w, 0.0)
        part.carry[p] = carry + s[:, TK:]
        part.acc[p] += lax.dot_general(w.astype(BF16), part.get_vt(p), NT_DIMS,
                                       preferred_element_type=F32)


def _carry_max(carry_refs):
    m = None
    for ref in carry_refs:
        for p in range(PAIRS):
            m = ref[p] if m is None else jnp.maximum(m, ref[p])
    return jnp.max(m)


def _sb_walk(first, n_blocks, m0, block_fn, carry_refs):
    def cond(c):
        j, m = c
        return jnp.logical_and(j < n_blocks, m > SKIP_LOG2)

    def body(c):
        j, _ = c
        block_fn(j)
        return j + 1, _carry_max(carry_refs)

    return lax.while_loop(cond, body, (first, m0))


def _sb_finish(h_ref, rows, acc_ref, mix_ref):
    n = acc_ref.shape[1] // 2
    lane_lo = _lane_lo(n)
    for p in range(PAIRS):
        o = jnp.where(lane_lo, acc_ref[p, 0:n], acc_ref[p, n:2 * n])
        g = h_ref[rows, COL_GA + p * LANES:COL_GA + (p + 1) * LANES]
        mix_ref[rows, COL_MIX_A + p * LANES:COL_MIX_A + (p + 1) * LANES] = (
            o * _silu(g)).astype(BF16)


def _stacked_index(n_rows):
    row = lax.broadcasted_iota(jnp.int32, (2 * n_rows, TK), 0)
    col = lax.broadcasted_iota(jnp.int32, (2 * n_rows, TK), 1)
    return jnp.where(row >= n_rows, row - n_rows, row), col


def _conv_module(h_ref, rows0, upad_ref, upad0, n_rows, mix_ref, cw_ref, cb_ref, g_ref, b_ref):
    for r in range(0, n_rows, CONV_ROWS):
        nr = min(CONV_ROWS, n_rows - r)
        c = jnp.broadcast_to(cb_ref[...], (nr, CONV_WIDTH))
        for s in range(SUBLANES):
            taps = [i for i in range(CONV_K) if (HALO_PAD + i) % SUBLANES == s]
            span = nr if s == 0 else nr + SUBLANES
            part = None
            for i in taps:
                base = upad0 + r + HALO_PAD + i - s
                term = cw_ref[i:i + 1, :] * upad_ref[pl.ds(base, span), :]
                part = term if part is None else part + term
            c = c + part[s:s + nr]
        mu = jnp.mean(c, axis=-1, keepdims=True)
        d = c - mu
        var = jnp.mean(d * d, axis=-1, keepdims=True)
        n = d * lax.rsqrt(var + LN_EPS) * g_ref[...] + b_ref[...]
        gate = h_ref[pl.ds(rows0 + r, nr), COL_GC:COL_GC + CONV_WIDTH]
        mix_ref[pl.ds(rows0 + r, nr), COL_MIX_C:COL_MIX_C + CONV_WIDTH] = (
            _silu(n) * _silu(gate)).astype(BF16)


def _mem_attend(h_ref, rows, n_rows, mkt, mvt, mix_ref):
    lane_lo = _lane_lo(n_rows)
    for p in range(MEM_HEADS // 2):
        qp = h_ref[rows, COL_QM + p * LANES:COL_QM + (p + 1) * LANES] * QK_SCALE
        mkp = mkt[p * LANES:(p + 1) * LANES, :]
        mvp = mvt[p * LANES:(p + 1) * LANES, :]
        outs = []
        for hh in range(2):
            qh = (jnp.where(lane_lo, qp, 0.0) if hh == 0 else jnp.where(lane_lo, 0.0, qp)).astype(BF16)
            s = jnp.dot(qh, mkp, preferred_element_type=F32)
            e = jnp.exp(s - jnp.max(s, axis=-1, keepdims=True))
            o = lax.dot_general(e.astype(BF16), mvp, NT_DIMS, preferred_element_type=F32)
            outs.append(o / jnp.sum(e, axis=-1, keepdims=True))
        o = jnp.where(lane_lo, outs[0], outs[1])
        g = h_ref[rows, COL_GM + p * LANES:COL_GM + (p + 1) * LANES]
        mix_ref[rows, COL_MIX_M + p * LANES:COL_MIX_M + (p + 1) * LANES] = (o * _silu(g)).astype(BF16)


def _out_norm(x, mix_ref, w_out_ref, g_ref, b_ref, alpha):
    r = alpha * x + jnp.dot(mix_ref[...], w_out_ref[...], preferred_element_type=F32)
    mu = jnp.mean(r, axis=-1, keepdims=True)
    d = r - mu
    var = jnp.mean(d * d, axis=-1, keepdims=True)
    return d * lax.rsqrt(var + LN_EPS) * g_ref[...] + b_ref[...]


def _prompt_kernel(x_ref, xn_ref, mem_ref, w_in_ref, w_kvt_ref, w_memt_ref, w_out_ref, u2_ref, cw_ref, cb_ref,
                   cg_ref, cbeta_ref, lg_ref, lb_ref,
                   y_ref, kt_ref, vt_ref, conv_ref, mkt_ref, mvt_ref,
                   h_scr, kt_scr, vt_scr, mkt_scr, mvt_scr, qm_scr, acc_scr, carry_scr,
                   upad_scr, mix_scr, *, tm, alpha):
    t = pl.program_id(1)
    n_sub = tm // TQ
    row, col = _stacked_index(TQ)
    causal = col < row

    @pl.when(t == 0)
    def _():
        kvt = lax.dot_general(w_memt_ref[...], mem_ref[0].astype(BF16), NT_DIMS,
                              preferred_element_type=F32)
        mkt_ref[0] = kvt[:MEM_WIDTH]
        mvt_ref[0] = kvt[MEM_WIDTH:]
        mkt_scr[...] = kvt[:MEM_WIDTH].astype(BF16)
        mvt_scr[...] = kvt[MEM_WIDTH:].astype(BF16)
        upad_scr[0:HALO, :] = jnp.zeros((HALO, CONV_WIDTH), F32)

    h_cur = h_scr.at[t % 2]
    h_next = h_scr.at[(t + 1) % 2]
    xb = x_ref[0].astype(BF16)
    xb_next = xn_ref[0].astype(BF16)

    @pl.when(t == 0)
    def _():
        _project(xb, w_in_ref, h_scr.at[0], 0, H_WIDTH)

    kvt = lax.dot_general(w_kvt_ref[...], xb, NT_DIMS, preferred_element_type=F32)
    kt_ref[0] = kvt[:SB_WIDTH]
    vt_ref[0] = kvt[SB_WIDTH:]
    blk0 = t * n_sub
    for c in range(n_sub):
        kt_scr[blk0 + c] = kvt[:SB_WIDTH, c * TK:(c + 1) * TK].astype(BF16)
        vt_scr[blk0 + c] = kvt[SB_WIDTH:, c * TK:(c + 1) * TK].astype(BF16)

    for i in range(n_sub):
        _build_qm(h_cur, slice(i * TQ, (i + 1) * TQ), qm_scr.at[i])
    acc_scr[...] = jnp.zeros(acc_scr.shape, F32)
    carry_scr[...] = jnp.zeros(carry_scr.shape, F32)
    carries = [carry_scr.at[i] for i in range(n_sub)]
    _project(xb_next, w_in_ref, h_next, 0, PROJ_SPLIT)

    def part(i, blk, mask):
        return _Part(qm_scr.at[i], acc_scr.at[i], carry_scr.at[i],
                     lambda p: kt_scr[blk, p * LANES:(p + 1) * LANES, :],
                     lambda p: vt_scr[blk, p * LANES:(p + 1) * LANES, :], mask)

    _sb_blocks([part(i, blk0 + i, causal) for i in range(n_sub)], u2_ref)
    j, m = _sb_walk(
        jnp.int32(0), blk0, jnp.float32(0.0),
        lambda j: _sb_blocks([part(i, blk0 + i - 1 - j, None) for i in range(n_sub)], u2_ref),
        carries)
    for d in range(n_sub - 1):
        def tail(d=d):
            _sb_blocks([part(i, i - 1 - d, None) for i in range(d + 1, n_sub)], u2_ref)
            return _carry_max(carries)
        m = lax.cond(jnp.logical_and(j >= blk0, m > SKIP_LOG2), tail, lambda m=m: m)
    for i in range(n_sub):
        _sb_finish(h_cur, slice(i * TQ, (i + 1) * TQ), acc_scr.at[i], mix_scr)

    _project(xb_next, w_in_ref, h_next, PROJ_SPLIT, H_WIDTH)
    u = h_cur[:, COL_CA:COL_CA + CONV_WIDTH] * _sigmoid(h_cur[:, COL_CB:COL_CB + CONV_WIDTH])
    upad_scr[HALO:HALO + tm, :] = u
    _conv_module(h_cur, 0, upad_scr, 0, tm, mix_scr, cw_ref, cb_ref, cg_ref, cbeta_ref)
    last_rows = upad_scr[tm:tm + HALO, :]
    upad_scr[0:HALO, :] = last_rows
    conv_ref[0] = last_rows

    _mem_attend(h_cur, slice(0, tm), tm, mkt_scr[...], mvt_scr[...], mix_scr)
    y_ref[0] = _out_norm(x_ref[0], mix_scr, w_out_ref, lg_ref, lb_ref, alpha)


def _const_spec(shape):
    return pl.BlockSpec(shape, lambda *_: (0,) * len(shape), pipeline_mode=pl.Buffered(1))


def _prompt_call(x, mem, w_in, w_kvt, w_memt, w_out, u2, cw, cb, cg, cbeta, lg, lb, alpha, tm):
    B, T, _ = x.shape
    assert T % tm == 0 and tm % TQ == 0 and TQ == TK
    grid = (B, T // tm)
    row_spec = lambda width: pl.BlockSpec((1, tm, width), lambda b, t: (b, t, 0))
    col_spec = lambda height: pl.BlockSpec((1, height, tm), lambda b, t: (b, 0, t))
    per_b = lambda r, width: pl.BlockSpec((1, r, width), lambda b, t: (b, 0, 0))
    out_shape = (
        jax.ShapeDtypeStruct((B, T, D_MODEL), F32),
        jax.ShapeDtypeStruct((B, SB_WIDTH, T), F32),
        jax.ShapeDtypeStruct((B, SB_WIDTH, T), F32),
        jax.ShapeDtypeStruct((B, HALO, CONV_WIDTH), F32),
        jax.ShapeDtypeStruct((B, MEM_WIDTH, N_MEM), F32),
        jax.ShapeDtypeStruct((B, MEM_WIDTH, N_MEM), F32),
    )
    scratch = [
        pltpu.VMEM((2, tm, H_WIDTH), F32),
        pltpu.VMEM((T // TK, SB_WIDTH, TK), BF16),
        pltpu.VMEM((T // TK, SB_WIDTH, TK), BF16),
        pltpu.VMEM((MEM_WIDTH, N_MEM), BF16),
        pltpu.VMEM((MEM_WIDTH, N_MEM), BF16),
        pltpu.VMEM((tm // TQ, PAIRS, 2 * TQ, LANES), BF16),
        pltpu.VMEM((tm // TQ, PAIRS, 2 * TQ, LANES), F32),
        pltpu.VMEM((tm // TQ, PAIRS, 2 * TQ, LANES), F32),
        pltpu.VMEM((HALO + tm, CONV_WIDTH), F32),
        pltpu.VMEM((tm, D_MODEL), BF16),
    ]
    return pl.pallas_call(
        functools.partial(_prompt_kernel, tm=tm, alpha=alpha),
        grid=grid,
        in_specs=[
            row_spec(D_MODEL),
            pl.BlockSpec((1, tm, D_MODEL), lambda b, t: (b, jnp.minimum(t + 1, T // tm - 1), 0)),
            per_b(N_MEM, D_MODEL),
            _const_spec((D_MODEL, H_WIDTH)),
            _const_spec((2 * SB_WIDTH, D_MODEL)),
            _const_spec((2 * MEM_WIDTH, D_MODEL)),
            _const_spec((D_MODEL, D_MODEL)),
            _const_spec((2 * TK, 2 * TK)),
            _const_spec((CONV_K, CONV_WIDTH)),
            _const_spec((1, CONV_WIDTH)),
            _const_spec((1, CONV_WIDTH)),
            _const_spec((1, CONV_WIDTH)),
            _const_spec((1, D_MODEL)),
            _const_spec((1, D_MODEL)),
        ],
        out_specs=(
            row_spec(D_MODEL), col_spec(SB_WIDTH), col_spec(SB_WIDTH),
            per_b(HALO, CONV_WIDTH), per_b(MEM_WIDTH, N_MEM), per_b(MEM_WIDTH, N_MEM),
        ),
        out_shape=out_shape,
        scratch_shapes=scratch,
        compiler_params=pltpu.CompilerParams(
            dimension_semantics=("arbitrary", "arbitrary"),
            vmem_limit_bytes=VMEM_LIMIT),
        name="prompt_layer",
    )(x, x, mem, w_in, w_kvt, w_memt, w_out, u2, cw, cb, cg, cbeta, lg, lb)


def _sample_kernel(x_ref, ktwin_ref, vtwin_ref, ktc_hbm, vtc_hbm, cpast_ref, mktc_ref, mvtc_ref,
                   w_in_ref, w_kvt_ref, w_out_ref, u2_ref, cw_ref, cb_ref, cg_ref, cbeta_ref,
                   lg_ref, lb_ref,
                   y_ref, k_ref, v_ref, conv_ref,
                   h_scr, ktnew_scr, vtnew_scr, ktbuf, vtbuf, sem, qm_scr, acc_scr, carry_scr,
                   upad_scr, mix_scr, *, group, n_new, win, past, alpha):
    step = pl.program_id(0)
    row, col = _stacked_index(n_new)
    n_win = win // TK
    n_blocks = past // TK

    xb = x_ref[...].astype(BF16)
    _project(xb, w_in_ref, h_scr, 0, IN_WIDTH)
    k_ref[...] = h_scr[:, COL_K:COL_K + SB_WIDTH]
    v_ref[...] = h_scr[:, COL_V:COL_V + SB_WIDTH]
    kvt = lax.dot_general(w_kvt_ref[...], xb, NT_DIMS, preferred_element_type=F32)
    ktnew_scr[...] = kvt[:SB_WIDTH].astype(BF16)
    vtnew_scr[...] = kvt[SB_WIDTH:].astype(BF16)

    for g in range(group):
        _build_qm(h_scr, slice(g * n_new, (g + 1) * n_new), qm_scr.at[g])
    acc_scr[...] = jnp.zeros(acc_scr.shape, F32)
    carry_scr[...] = jnp.zeros(carry_scr.shape, F32)
    carries = [carry_scr.at[g] for g in range(group)]

    def part(g, get_kt, get_vt, mask):
        return _Part(qm_scr.at[g], acc_scr.at[g], carry_scr.at[g], get_kt, get_vt, mask)

    def own_past(g):
        rel = col - g * n_new
        return jnp.logical_and(rel >= 0, rel < row)

    _sb_blocks([part(g, lambda p: ktnew_scr[p * LANES:(p + 1) * LANES, :],
                     lambda p: vtnew_scr[p * LANES:(p + 1) * LANES, :], own_past(g))
                for g in range(group)], u2_ref)

    def win_blocks(j):
        cols = slice(win - (j + 1) * TK, win - j * TK)
        _sb_blocks([part(g, lambda p, g=g: ktwin_ref[g, p * LANES:(p + 1) * LANES, cols].astype(BF16),
                         lambda p, g=g: vtwin_ref[g, p * LANES:(p + 1) * LANES, cols].astype(BF16),
                         None)
                    for g in range(group)], u2_ref)
        return _carry_max(carries)

    m = jnp.float32(0.0)
    for j in range(n_win):
        m = lax.cond(m > SKIP_LOG2, functools.partial(win_blocks, j), lambda m=m: m)

    def per_stream(g, _):
        rows = pl.ds(pl.multiple_of(g * n_new, n_new), n_new)
        stream = step * group + g

        def far_block(j):
            start = pl.multiple_of(past - (j + 1) * TK, TK)
            ck = pltpu.make_async_copy(ktc_hbm.at[stream, :, pl.ds(start, TK)], ktbuf, sem.at[0])
            cv = pltpu.make_async_copy(vtc_hbm.at[stream, :, pl.ds(start, TK)], vtbuf, sem.at[1])
            ck.start()
            cv.start()
            ck.wait()
            cv.wait()
            _sb_blocks([part(g, lambda p: ktbuf[p * LANES:(p + 1) * LANES, :].astype(BF16),
                             lambda p: vtbuf[p * LANES:(p + 1) * LANES, :].astype(BF16), None)],
                       u2_ref)

        own = [carry_scr.at[g]]
        _sb_walk(jnp.int32(n_win), n_blocks, _carry_max(own), far_block, own)
        _sb_finish(h_scr, rows, acc_scr.at[g], mix_scr)
        _mem_attend(h_scr, rows, n_new, mktc_ref[g].astype(BF16), mvtc_ref[g].astype(BF16), mix_scr)
        return 0

    lax.fori_loop(0, group, per_stream, 0)

    u = h_scr[:, COL_CA:COL_CA + CONV_WIDTH] * _sigmoid(h_scr[:, COL_CB:COL_CB + CONV_WIDTH])
    for g in range(group):
        base = g * (HALO + n_new)
        upad_scr[base:base + HALO, :] = cpast_ref[g]
        upad_scr[base + HALO:base + HALO + n_new, :] = u[g * n_new:(g + 1) * n_new]
        _conv_module(h_scr, g * n_new, upad_scr, base, n_new, mix_scr, cw_ref, cb_ref, cg_ref, cbeta_ref)
        conv_ref[g] = upad_scr[base + n_new:base + n_new + HALO, :]

    y_ref[...] = _out_norm(x_ref[...], mix_scr, w_out_ref, lg_ref, lb_ref, alpha)


def _sample_call(x, ktc, vtc, cpast, mktc, mvtc, w_in, w_kvt, w_out, u2, cw, cb, cg, cbeta, lg, lb,
                 alpha, win):
    S, n_new, _ = x.shape
    past = ktc.shape[2]
    group = TK // n_new
    assert group * n_new == TK and S % group == 0 and past % win == 0 and win % TK == 0
    assert n_new % 16 == 0 and n_new >= CONV_STATE
    rows = group * n_new
    x2 = x.reshape(S * n_new, D_MODEL)
    row_spec = lambda width: pl.BlockSpec((rows, width), lambda i: (i, 0))
    grp = lambda r, width: pl.BlockSpec((group, r, width), lambda i: (i, 0, 0))
    win_spec = pl.BlockSpec((group, SB_WIDTH, win), lambda i: (i, 0, past // win - 1))
    any_spec = pl.BlockSpec(memory_space=pl.ANY)
    out_shape = (
        jax.ShapeDtypeStruct((S * n_new, D_MODEL), F32),
        jax.ShapeDtypeStruct((S * n_new, SB_WIDTH), F32),
        jax.ShapeDtypeStruct((S * n_new, SB_WIDTH), F32),
        jax.ShapeDtypeStruct((S, HALO, CONV_WIDTH), F32),
    )
    scratch = [
        pltpu.VMEM((rows, IN_WIDTH), F32),
        pltpu.VMEM((SB_WIDTH, TK), BF16),
        pltpu.VMEM((SB_WIDTH, TK), BF16),
        pltpu.VMEM((SB_WIDTH, TK), F32),
        pltpu.VMEM((SB_WIDTH, TK), F32),
        pltpu.SemaphoreType.DMA((2,)),
        pltpu.VMEM((group, PAIRS, 2 * n_new, LANES), BF16),
        pltpu.VMEM((group, PAIRS, 2 * n_new, LANES), F32),
        pltpu.VMEM((group, PAIRS, 2 * n_new, LANES), F32),
        pltpu.VMEM((group * (HALO + n_new), CONV_WIDTH), F32),
        pltpu.VMEM((rows, D_MODEL), BF16),
    ]
    return pl.pallas_call(
        functools.partial(_sample_kernel, group=group, n_new=n_new, win=win, past=past, alpha=alpha),
        grid=(S // group,),
        in_specs=[
            row_spec(D_MODEL), win_spec, win_spec, any_spec, any_spec,
            grp(HALO, CONV_WIDTH), grp(MEM_WIDTH, N_MEM), grp(MEM_WIDTH, N_MEM),
            _const_spec((D_MODEL, IN_WIDTH)),
            _const_spec((2 * SB_WIDTH, D_MODEL)),
            _const_spec((D_MODEL, D_MODEL)),
            _const_spec((2 * TK, 2 * TK)),
            _const_spec((CONV_K, CONV_WIDTH)),
            _const_spec((1, CONV_WIDTH)),
            _const_spec((1, CONV_WIDTH)),
            _const_spec((1, CONV_WIDTH)),
            _const_spec((1, D_MODEL)),
            _const_spec((1, D_MODEL)),
        ],
        out_specs=(row_spec(D_MODEL), row_spec(SB_WIDTH), row_spec(SB_WIDTH), grp(HALO, CONV_WIDTH)),
        out_shape=out_shape,
        scratch_shapes=scratch,
        compiler_params=pltpu.CompilerParams(
            dimension_semantics=("arbitrary",),
            vmem_limit_bytes=VMEM_LIMIT),
        name="sample_layer",
    )(x2, ktc, vtc, ktc, vtc, cpast, mktc, mvtc, w_in, w_kvt, w_out, u2, cw, cb, cg, cbeta, lg, lb)


def _prefix_matrix():
    j = lax.broadcasted_iota(jnp.int32, (TK, TK), 0)
    s = lax.broadcasted_iota(jnp.int32, (TK, TK), 1)
    half = jnp.concatenate([(j > s).astype(BF16), jnp.ones((TK, TK), BF16)], axis=1)
    return jnp.concatenate([half, half], axis=0)


def _time_minor(a):
    n, time, heads, dim = a.shape
    return jnp.transpose(a, (0, 2, 3, 1)).reshape(n, heads * dim, time)


def _time_major(a, heads):
    n, width, time = a.shape
    return jnp.transpose(a.reshape(n, heads, width // heads, time), (0, 3, 1, 2))[None]


def kernel(x_prompt, x_sample, cache_sb_k, cache_sb_v, cache_conv, cache_mem_k, cache_mem_v, mem_prompt, w_in, w_mem_kv, conv_w, conv_b, conv_ln_g, conv_ln_b, w_out, ln_g, ln_b):
    depth = w_in.shape[0]
    assert depth == 1, "single layer only"
    alpha = (2 * depth) ** 0.25
    S, n_new, _ = x_sample.shape

    u2 = _prefix_matrix()
    w = w_in[0]
    w_kv = w[:, REF_K:REF_V_END]
    w_in_b = jnp.concatenate([w[:, :REF_K], w[:, REF_V_END:], w_kv], axis=1).astype(BF16)
    w_kvt = w_kv.T.astype(BF16)
    w_memt = w_mem_kv[0].T.astype(BF16)
    w_out_b = w_out[0].astype(BF16)
    cw = conv_w[0]
    cb = conv_b[0].reshape(1, CONV_WIDTH)
    cg = conv_ln_g[0].reshape(1, CONV_WIDTH)
    cbeta = conv_ln_b[0].reshape(1, CONV_WIDTH)
    lg = ln_g[0].reshape(1, D_MODEL)
    lb = ln_b[0].reshape(1, D_MODEL)

    yp, ktp, vtp, cp, mktp, mvtp = _prompt_call(
        x_prompt, mem_prompt, w_in_b, w_kvt, w_memt, w_out_b, u2, cw, cb, cg, cbeta, lg, lb,
        alpha, tm=256)

    cpast = jnp.pad(cache_conv[0], ((0, 0), (HALO_PAD, 0), (0, 0)))
    ys, ks, vs, cs = _sample_call(
        x_sample, _time_minor(cache_sb_k[0]), _time_minor(cache_sb_v[0]), cpast,
        _time_minor(cache_mem_k[0]), _time_minor(cache_mem_v[0]),
        w_in_b, w_kvt, w_out_b, u2, cw, cb, cg, cbeta, lg, lb, alpha, win=256)

    return (
        yp,
        ys.reshape(S, n_new, D_MODEL),
        _time_major(ktp, SB_HEADS),
        _time_major(vtp, SB_HEADS),
        cp[None, :, HALO_PAD:, :],
        _time_major(mktp, MEM_HEADS),
        _time_major(mvtp, MEM_HEADS),
        ks.reshape(1, S, n_new, SB_HEADS, HEAD_DIM),
        vs.reshape(1, S, n_new, SB_HEADS, HEAD_DIM),
        cs[None, :, HALO_PAD:, :],
    )
```

```python
import functools
import math
from typing import Any, Callable, NamedTuple

import jax
import jax.numpy as jnp
from jax import lax
from jax.experimental import pallas as pl
from jax.experimental.pallas import tpu as pltpu

F32 = jnp.float32
BF16 = jnp.bfloat16

D_MODEL = 1024
SB_HEADS = 8
HEAD_DIM = 64
SB_WIDTH = SB_HEADS * HEAD_DIM
CONV_WIDTH = 256
CONV_K = 31
CONV_STATE = CONV_K - 1
MEM_HEADS = 4
MEM_WIDTH = MEM_HEADS * HEAD_DIM
N_MEM = 256
IN_WIDTH = 4 * SB_WIDTH + 3 * CONV_WIDTH + 2 * MEM_WIDTH
REF_K, REF_V_END = SB_WIDTH, 3 * SB_WIDTH
COL_Q = 0
COL_GA = SB_WIDTH
COL_CA = COL_GA + SB_WIDTH
COL_CB = COL_CA + CONV_WIDTH
COL_GC = COL_CB + CONV_WIDTH
COL_QM = COL_GC + CONV_WIDTH
COL_GM = COL_QM + MEM_WIDTH
H_WIDTH = COL_GM + MEM_WIDTH
COL_K = H_WIDTH
COL_V = COL_K + SB_WIDTH
COL_MIX_A, COL_MIX_C, COL_MIX_M = 0, SB_WIDTH, SB_WIDTH + CONV_WIDTH
LN_EPS = 1e-5
QK_SCALE = HEAD_DIM ** -0.5

LANES = 128
SUBLANES = 8
PAIRS = SB_HEADS // 2
TQ = 128
TK = 128
HALO = 32
HALO_PAD = HALO - CONV_STATE
CONV_ROWS = 64
PROJ_COLS = 256
WALK_STRIDE = 2
PROJ_SPLIT = 1024
INV_LN2 = 1.0 / math.log(2.0)
QK_SCALE_LOG2 = QK_SCALE * INV_LN2
SKIP_LOG2 = -160.0
VMEM_LIMIT = 60 * 1024 * 1024

NT_DIMS = (((1,), (1,)), ((), ()))


def _sigmoid(x):
    return 1.0 / (1.0 + jnp.exp(-x))


def _silu(x):
    return x * _sigmoid(x)


def _project(xb, w_in_ref, h_ref, first_col, last_col):
    for c in range(first_col // PROJ_COLS, last_col // PROJ_COLS):
        cols = slice(c * PROJ_COLS, (c + 1) * PROJ_COLS)
        h_ref[:, cols] = jnp.dot(xb, w_in_ref[:, cols], preferred_element_type=F32)


def _lane_lo(n_rows):
    return lax.broadcasted_iota(jnp.int32, (n_rows, LANES), 1) < HEAD_DIM


def _build_qm(h_ref, rows, qm_ref):
    n = qm_ref.shape[1] // 2
    lane_lo = _lane_lo(n)
    for p in range(PAIRS):
        qp = h_ref[rows, COL_Q + p * LANES:COL_Q + (p + 1) * LANES] * QK_SCALE_LOG2
        qm_ref[p, 0:n] = jnp.where(lane_lo, qp, 0.0).astype(BF16)
        qm_ref[p, n:2 * n] = jnp.where(lane_lo, 0.0, qp).astype(BF16)


class _Part(NamedTuple):
    qm: Any
    acc: Any
    carry: Any
    get_kt: Callable[[int], Any]
    get_vt: Callable[[int], Any]
    mask: Any


def _sb_blocks(parts, u2_ref, after_scores=None):
    chains = [(part, p) for part in parts for p in range(PAIRS)]
    n2 = parts[0].qm.shape[1]
    zs = [jnp.dot(part.qm[p], part.get_kt(p), preferred_element_type=F32) for part, p in chains]
    if after_scores is not None:
        after_scores()
    hits, split = [], []
    for (part, _), z in zip(chains, zs):
        nz = -z
        soft = jnp.log(1.0 + jnp.exp2(jnp.minimum(z, nz))) * INV_LN2
        log_fail = jnp.minimum(nz, 0.0) - soft
        hits.append(log_fail + z)
        if part.mask is not None:
            log_fail = jnp.where(part.mask, log_fail, 0.0)
        split.append(log_fail.astype(BF16))
    sums = jnp.dot(jnp.concatenate(split, axis=0), u2_ref[...],
                   preferred_element_type=F32)
    for i, ((part, p), log_hit) in enumerate(zip(chains, hits)):
        s = sums[i * n2:(i + 1) * n2]
        carry = part.carry[p]
        w = jnp.exp2(log_hit + s[:, :TK] + carry)
        if part.mask is not None:
            w = jnp.where(part.mask, w, 0.0)
        part.carry[p] = carry + s[:, TK:]
        part.acc[p] += lax.dot_general(w.astype(BF16), part.get_vt(p), NT_DIMS,
                                       preferred_element_type=F32)


def _carry_max(carry_refs):
    m = None
    for ref in carry_refs:
        for p in range(PAIRS):
            m = ref[p] if m is None else jnp.maximum(m, ref[p])
    return jnp.max(m)


def _sb_walk(first, n_blocks, m0, block_fn, carry_refs, stride=1):
    def cond(c):
        j, m = c
        return jnp.logical_and(j + stride <= n_blocks, m > SKIP_LOG2)

    def body(c):
        j, _ = c
        block_fn(j)
        return j + stride, _carry_max(carry_refs)

    return lax.while_loop(cond, body, (first, m0))


def _sb_finish(h_ref, rows, acc_ref, mix_ref):
    n = acc_ref.shape[1] // 2
    lane_lo = _lane_lo(n)
    for p in range(PAIRS):
        o = jnp.where(lane_lo, acc_ref[p, 0:n], acc_ref[p, n:2 * n])
        g = h_ref[rows, COL_GA + p * LANES:COL_GA + (p + 1) * LANES]
        mix_ref[rows, COL_MIX_A + p * LANES:COL_MIX_A + (p + 1) * LANES] = (
            o * _silu(g)).astype(BF16)


def _stacked_index(n_rows):
    row = lax.broadcasted_iota(jnp.int32, (2 * n_rows, TK), 0)
    col = lax.broadcasted_iota(jnp.int32, (2 * n_rows, TK), 1)
    return jnp.where(row >= n_rows, row - n_rows, row), col


def _conv_module(h_ref, rows0, upad_ref, upad0, n_rows, mix_ref, cw_ref, cb_ref, g_ref, b_ref):
    for r in range(0, n_rows, CONV_ROWS):
        nr = min(CONV_ROWS, n_rows - r)
        c = jnp.broadcast_to(cb_ref[...], (nr, CONV_WIDTH))
        for s in range(SUBLANES):
            taps = [i for i in range(CONV_K) if (HALO_PAD + i) % SUBLANES == s]
            span = nr if s == 0 else nr + SUBLANES
            part = None
            for i in taps:
                base = upad0 + r + HALO_PAD + i - s
                term = cw_ref[i:i + 1, :] * upad_ref[pl.ds(base, span), :]
                part = term if part is None else part + term
            c = c + part[s:s + nr]
        mu = jnp.mean(c, axis=-1, keepdims=True)
        d = c - mu
        var = jnp.mean(d * d, axis=-1, keepdims=True)
        n = d * lax.rsqrt(var + LN_EPS) * g_ref[...] + b_ref[...]
        gate = h_ref[pl.ds(rows0 + r, nr), COL_GC:COL_GC + CONV_WIDTH]
        mix_ref[pl.ds(rows0 + r, nr), COL_MIX_C:COL_MIX_C + CONV_WIDTH] = (
            _silu(n) * _silu(gate)).astype(BF16)


def _mem_attend(h_ref, rows, n_rows, mkt, mvt, mix_ref):
    lane_lo = _lane_lo(n_rows)
    for p in range(MEM_HEADS // 2):
        qp = h_ref[rows, COL_QM + p * LANES:COL_QM + (p + 1) * LANES] * QK_SCALE
        mkp = mkt[p * LANES:(p + 1) * LANES, :]
        mvp = mvt[p * LANES:(p + 1) * LANES, :]
        outs = []
        for hh in range(2):
            qh = (jnp.where(lane_lo, qp, 0.0) if hh == 0 else jnp.where(lane_lo, 0.0, qp)).astype(BF16)
            s = jnp.dot(qh, mkp, preferred_element_type=F32)
            e = jnp.exp(s - jnp.max(s, axis=-1, keepdims=True))
            o = lax.dot_general(e.astype(BF16), mvp, NT_DIMS, preferred_element_type=F32)
            outs.append(o / jnp.sum(e, axis=-1, keepdims=True))
        o = jnp.where(lane_lo, outs[0], outs[1])
        g = h_ref[rows, COL_GM + p * LANES:COL_GM + (p + 1) * LANES]
        mix_ref[rows, COL_MIX_M + p * LANES:COL_MIX_M + (p + 1) * LANES] = (o * _silu(g)).astype(BF16)


def _out_norm(x, mix_ref, w_out_ref, g_ref, b_ref, alpha):
    r = alpha * x + jnp.dot(mix_ref[...], w_out_ref[...], preferred_element_type=F32)
    mu = jnp.mean(r, axis=-1, keepdims=True)
    d = r - mu
    var = jnp.mean(d * d, axis=-1, keepdims=True)
    return d * lax.rsqrt(var + LN_EPS) * g_ref[...] + b_ref[...]


def _prompt_kernel(x_ref, xn_ref, mem_ref, w_in_ref, w_kvt_ref, w_memt_ref, w_out_ref, u2_ref, cw_ref, cb_ref,
                   cg_ref, cbeta_ref, lg_ref, lb_ref,
                   y_ref, kt_ref, vt_ref, conv_ref, mkt_ref, mvt_ref,
                   h_scr, kt_scr, vt_scr, mkt_scr, mvt_scr, qm_scr, acc_scr, carry_scr,
                   upad_scr, mix_scr, *, tm, alpha):
    t = pl.program_id(1)
    n_sub = tm // TQ
    row, col = _stacked_index(TQ)
    causal = col < row

    @pl.when(t == 0)
    def _():
        kvt = lax.dot_general(w_memt_ref[...], mem_ref[0].astype(BF16), NT_DIMS,
                              preferred_element_type=F32)
        mkt_ref[0] = kvt[:MEM_WIDTH]
        mvt_ref[0] = kvt[MEM_WIDTH:]
        mkt_scr[...] = kvt[:MEM_WIDTH].astype(BF16)
        mvt_scr[...] = kvt[MEM_WIDTH:].astype(BF16)
        upad_scr[0:HALO, :] = jnp.zeros((HALO, CONV_WIDTH), F32)

    h_cur = h_scr.at[t % 2]
    h_next = h_scr.at[(t + 1) % 2]
    xb = x_ref[0].astype(BF16)
    xb_next = xn_ref[0].astype(BF16)

    @pl.when(t == 0)
    def _():
        _project(xb, w_in_ref, h_scr.at[0], 0, H_WIDTH)

    kvt = lax.dot_general(w_kvt_ref[...], xb, NT_DIMS, preferred_element_type=F32)
    kt_ref[0] = kvt[:SB_WIDTH]
    vt_ref[0] = kvt[SB_WIDTH:]
    blk0 = t * n_sub
    for c in range(n_sub):
        kt_scr[blk0 + c] = kvt[:SB_WIDTH, c * TK:(c + 1) * TK].astype(BF16)
        vt_scr[blk0 + c] = kvt[SB_WIDTH:, c * TK:(c + 1) * TK].astype(BF16)

    for i in range(n_sub):
        _build_qm(h_cur, slice(i * TQ, (i + 1) * TQ), qm_scr.at[i])
    acc_scr[...] = jnp.zeros(acc_scr.shape, F32)
    carry_scr[...] = jnp.zeros(carry_scr.shape, F32)
    carries = [carry_scr.at[i] for i in range(n_sub)]

    def part(i, blk, mask):
        return _Part(qm_scr.at[i], acc_scr.at[i], carry_scr.at[i],
                     lambda p: kt_scr[blk, p * LANES:(p + 1) * LANES, :],
                     lambda p: vt_scr[blk, p * LANES:(p + 1) * LANES, :], mask)

    _project(xb_next, w_in_ref, h_next, 0, PROJ_SPLIT)
    _sb_blocks([part(i, blk0 + i, causal) for i in range(n_sub)], u2_ref)
    def walk(j, stride):
        _sb_blocks([part(i, blk0 + i - 1 - j - k, None)
                    for k in range(stride) for i in range(n_sub)], u2_ref)

    j, m = _sb_walk(jnp.int32(0), blk0, jnp.float32(0.0),
                    functools.partial(walk, stride=WALK_STRIDE), carries, WALK_STRIDE)
    j, m = _sb_walk(j, blk0, m, functools.partial(walk, stride=1), carries)
    for d in range(n_sub - 1):
        def tail(d=d):
            _sb_blocks([part(i, i - 1 - d, None) for i in range(d + 1, n_sub)], u2_ref)
            return _carry_max(carries)
        m = lax.cond(jnp.logical_and(j >= blk0, m > SKIP_LOG2), tail, lambda m=m: m)
    for i in range(n_sub):
        _sb_finish(h_cur, slice(i * TQ, (i + 1) * TQ), acc_scr.at[i], mix_scr)

    _project(xb_next, w_in_ref, h_next, PROJ_SPLIT, H_WIDTH)
    u = h_cur[:, COL_CA:COL_CA + CONV_WIDTH] * _sigmoid(h_cur[:, COL_CB:COL_CB + CONV_WIDTH])
    upad_scr[HALO:HALO + tm, :] = u
    _conv_module(h_cur, 0, upad_scr, 0, tm, mix_scr, cw_ref, cb_ref, cg_ref, cbeta_ref)
    last_rows = upad_scr[tm:tm + HALO, :]
    upad_scr[0:HALO, :] = last_rows
    conv_ref[0] = last_rows

    _mem_attend(h_cur, slice(0, tm), tm, mkt_scr[...], mvt_scr[...], mix_scr)
    y_ref[0] = _out_norm(x_ref[0], mix_scr, w_out_ref, lg_ref, lb_ref, alpha)


def _const_spec(shape):
    return pl.BlockSpec(shape, lambda *_: (0,) * len(shape), pipeline_mode=pl.Buffered(1))


def _prompt_call(x, mem, w_in, w_kvt, w_memt, w_out, u2, cw, cb, cg, cbeta, lg, lb, alpha, tm):
    B, T, _ = x.shape
    assert T % tm == 0 and tm % TQ == 0 and TQ == TK
    grid = (B, T // tm)
    row_spec = lambda width: pl.BlockSpec((1, tm, width), lambda b, t: (b, t, 0))
    col_spec = lambda height: pl.BlockSpec((1, height, tm), lambda b, t: (b, 0, t))
    per_b = lambda r, width: pl.BlockSpec((1, r, width), lambda b, t: (b, 0, 0))
    out_shape = (
        jax.ShapeDtypeStruct((B, T, D_MODEL), F32),
        jax.ShapeDtypeStruct((B, SB_WIDTH, T), F32),
        jax.ShapeDtypeStruct((B, SB_WIDTH, T), F32),
        jax.ShapeDtypeStruct((B, HALO, CONV_WIDTH), F32),
        jax.ShapeDtypeStruct((B, MEM_WIDTH, N_MEM), F32),
        jax.ShapeDtypeStruct((B, MEM_WIDTH, N_MEM), F32),
    )
    scratch = [
        pltpu.VMEM((2, tm, H_WIDTH), F32),
        pltpu.VMEM((T // TK, SB_WIDTH, TK), BF16),
        pltpu.VMEM((T // TK, SB_WIDTH, TK), BF16),
        pltpu.VMEM((MEM_WIDTH, N_MEM), BF16),
        pltpu.VMEM((MEM_WIDTH, N_MEM), BF16),
        pltpu.VMEM((tm // TQ, PAIRS, 2 * TQ, LANES), BF16),
        pltpu.VMEM((tm // TQ, PAIRS, 2 * TQ, LANES), F32),
        pltpu.VMEM((tm // TQ, PAIRS, 2 * TQ, LANES), F32),
        pltpu.VMEM((HALO + tm, CONV_WIDTH), F32),
        pltpu.VMEM((tm, D_MODEL), BF16),
    ]
    return pl.pallas_call(
        functools.partial(_prompt_kernel, tm=tm, alpha=alpha),
        grid=grid,
        in_specs=[
            row_spec(D_MODEL),
            pl.BlockSpec((1, tm, D_MODEL), lambda b, t: (b, jnp.minimum(t + 1, T // tm - 1), 0)),
            per_b(N_MEM, D_MODEL),
            _const_spec((D_MODEL, H_WIDTH)),
            _const_spec((2 * SB_WIDTH, D_MODEL)),
            _const_spec((2 * MEM_WIDTH, D_MODEL)),
            _const_spec((D_MODEL, D_MODEL)),
            _const_spec((TK, 2 * TK)),
            _const_spec((CONV_K, CONV_WIDTH)),
            _const_spec((1, CONV_WIDTH)),
            _const_spec((1, CONV_WIDTH)),
            _const_spec((1, CONV_WIDTH)),
            _const_spec((1, D_MODEL)),
            _const_spec((1, D_MODEL)),
        ],
        out_specs=(
            row_spec(D_MODEL), col_spec(SB_WIDTH), col_spec(SB_WIDTH),
            per_b(HALO, CONV_WIDTH), per_b(MEM_WIDTH, N_MEM), per_b(MEM_WIDTH, N_MEM),
        ),
        out_shape=out_shape,
        scratch_shapes=scratch,
        compiler_params=pltpu.CompilerParams(
            dimension_semantics=("arbitrary", "arbitrary"),
            vmem_limit_bytes=VMEM_LIMIT),
        name="prompt_layer",
    )(x, x, mem, w_in, w_kvt, w_memt, w_out, u2, cw, cb, cg, cbeta, lg, lb)


def _sample_kernel(x_ref, ktwin_ref, vtwin_ref, ktc_hbm, vtc_hbm, cpast_ref, mktc_ref, mvtc_ref,
                   w_in_ref, w_kvt_ref, w_out_ref, u2_ref, cw_ref, cb_ref, cg_ref, cbeta_ref,
                   lg_ref, lb_ref,
                   y_ref, k_ref, v_ref, conv_ref,
                   h_scr, ktnew_scr, vtnew_scr, ktbuf, vtbuf, sem, qm_scr, acc_scr, carry_scr,
                   upad_scr, mix_scr, *, group, n_new, win, past, alpha):
    step = pl.program_id(0)
    row, col = _stacked_index(n_new)
    n_win = win // TK
    n_blocks = past // TK

    xb = x_ref[...].astype(BF16)
    _project(xb, w_in_ref, h_scr, 0, IN_WIDTH)
    k_ref[...] = h_scr[:, COL_K:COL_K + SB_WIDTH]
    v_ref[...] = h_scr[:, COL_V:COL_V + SB_WIDTH]
    kvt = lax.dot_general(w_kvt_ref[...], xb, NT_DIMS, preferred_element_type=F32)
    ktnew_scr[...] = kvt[:SB_WIDTH].astype(BF16)
    vtnew_scr[...] = kvt[SB_WIDTH:].astype(BF16)

    for g in range(group):
        _build_qm(h_scr, slice(g * n_new, (g + 1) * n_new), qm_scr.at[g])
    acc_scr[...] = jnp.zeros(acc_scr.shape, F32)
    carry_scr[...] = jnp.zeros(carry_scr.shape, F32)
    carries = [carry_scr.at[g] for g in range(group)]

    def part(g, get_kt, get_vt, mask):
        return _Part(qm_scr.at[g], acc_scr.at[g], carry_scr.at[g], get_kt, get_vt, mask)

    def own_past(g):
        rel = col - g * n_new
        return jnp.logical_and(rel >= 0, rel < row)

    _sb_blocks([part(g, lambda p: ktnew_scr[p * LANES:(p + 1) * LANES, :],
                     lambda p: vtnew_scr[p * LANES:(p + 1) * LANES, :], own_past(g))
                for g in range(group)], u2_ref)

    def win_blocks(j):
        cols = slice(win - (j + 1) * TK, win - j * TK)
        _sb_blocks([part(g, lambda p, g=g: ktwin_ref[g, p * LANES:(p + 1) * LANES, cols].astype(BF16),
                         lambda p, g=g: vtwin_ref[g, p * LANES:(p + 1) * LANES, cols].astype(BF16),
                         None)
                    for g in range(group)], u2_ref)
        return _carry_max(carries)

    m = jnp.float32(0.0)
    for j in range(n_win):
        m = lax.cond(m > SKIP_LOG2, functools.partial(win_blocks, j), lambda m=m: m)

    def per_stream(g, _):
        rows = pl.ds(pl.multiple_of(g * n_new, n_new), n_new)
        stream = step * group + g

        def far_block(j):
            start = pl.multiple_of(past - (j + 1) * TK, TK)
            ck = pltpu.make_async_copy(ktc_hbm.at[stream, :, pl.ds(start, TK)], ktbuf, sem.at[0])
            cv = pltpu.make_async_copy(vtc_hbm.at[stream, :, pl.ds(start, TK)], vtbuf, sem.at[1])
            ck.start()
            cv.start()
            ck.wait()
            cv.wait()
            _sb_blocks([part(g, lambda p: ktbuf[p * LANES:(p + 1) * LANES, :].astype(BF16),
                             lambda p: vtbuf[p * LANES:(p + 1) * LANES, :].astype(BF16), None)],
                       u2_ref)

        own = [carry_scr.at[g]]
        _sb_walk(jnp.int32(n_win), n_blocks, _carry_max(own), far_block, own)
        _sb_finish(h_scr, rows, acc_scr.at[g], mix_scr)
        _mem_attend(h_scr, rows, n_new, mktc_ref[g].astype(BF16), mvtc_ref[g].astype(BF16), mix_scr)
        return 0

    lax.fori_loop(0, group, per_stream, 0)

    u = h_scr[:, COL_CA:COL_CA + CONV_WIDTH] * _sigmoid(h_scr[:, COL_CB:COL_CB + CONV_WIDTH])
    for g in range(group):
        base = g * (HALO + n_new)
        upad_scr[base:base + HALO, :] = cpast_ref[g]
        upad_scr[base + HALO:base + HALO + n_new, :] = u[g * n_new:(g + 1) * n_new]
        _conv_module(h_scr, g * n_new, upad_scr, base, n_new, mix_scr, cw_ref, cb_ref, cg_ref, cbeta_ref)
        conv_ref[g] = upad_scr[base + n_new:base + n_new + HALO, :]

    y_ref[...] = _out_norm(x_ref[...], mix_scr, w_out_ref, lg_ref, lb_ref, alpha)


def _sample_call(x, ktc, vtc, cpast, mktc, mvtc, w_in, w_kvt, w_out, u2, cw, cb, cg, cbeta, lg, lb,
                 alpha, win):
    S, n_new, _ = x.shape
    past = ktc.shape[2]
    group = TK // n_new
    assert group * n_new == TK and S % group == 0 and past % win == 0 and win % TK == 0
    assert n_new % 16 == 0 and n_new >= CONV_STATE
    rows = group * n_new
    x2 = x.reshape(S * n_new, D_MODEL)
    row_spec = lambda width: pl.BlockSpec((rows, width), lambda i: (i, 0))
    grp = lambda r, width: pl.BlockSpec((group, r, width), lambda i: (i, 0, 0))
    win_spec = pl.BlockSpec((group, SB_WIDTH, win), lambda i: (i, 0, past // win - 1))
    any_spec = pl.BlockSpec(memory_space=pl.ANY)
    out_shape = (
        jax.ShapeDtypeStruct((S * n_new, D_MODEL), F32),
        jax.ShapeDtypeStruct((S * n_new, SB_WIDTH), F32),
        jax.ShapeDtypeStruct((S * n_new, SB_WIDTH), F32),
        jax.ShapeDtypeStruct((S, HALO, CONV_WIDTH), F32),
    )
    scratch = [
        pltpu.VMEM((rows, IN_WIDTH), F32),
        pltpu.VMEM((SB_WIDTH, TK), BF16),
        pltpu.VMEM((SB_WIDTH, TK), BF16),
        pltpu.VMEM((SB_WIDTH, TK), F32),
        pltpu.VMEM((SB_WIDTH, TK), F32),
        pltpu.SemaphoreType.DMA((2,)),
        pltpu.VMEM((group, PAIRS, 2 * n_new, LANES), BF16),
        pltpu.VMEM((group, PAIRS, 2 * n_new, LANES), F32),
        pltpu.VMEM((group, PAIRS, 2 * n_new, LANES), F32),
        pltpu.VMEM((group * (HALO + n_new), CONV_WIDTH), F32),
        pltpu.VMEM((rows, D_MODEL), BF16),
    ]
    return pl.pallas_call(
        functools.partial(_sample_kernel, group=group, n_new=n_new, win=win, past=past, alpha=alpha),
        grid=(S // group,),
        in_specs=[
            row_spec(D_MODEL), win_spec, win_spec, any_spec, any_spec,
            grp(HALO, CONV_WIDTH), grp(MEM_WIDTH, N_MEM), grp(MEM_WIDTH, N_MEM),
            _const_spec((D_MODEL, IN_WIDTH)),
            _const_spec((2 * SB_WIDTH, D_MODEL)),
            _const_spec((D_MODEL, D_MODEL)),
            _const_spec((TK, 2 * TK)),
            _const_spec((CONV_K, CONV_WIDTH)),
            _const_spec((1, CONV_WIDTH)),
            _const_spec((1, CONV_WIDTH)),
            _const_spec((1, CONV_WIDTH)),
            _const_spec((1, D_MODEL)),
            _const_spec((1, D_MODEL)),
        ],
        out_specs=(row_spec(D_MODEL), row_spec(SB_WIDTH), row_spec(SB_WIDTH), grp(HALO, CONV_WIDTH)),
        out_shape=out_shape,
        scratch_shapes=scratch,
        compiler_params=pltpu.CompilerParams(
            dimension_semantics=("arbitrary",),
            vmem_limit_bytes=VMEM_LIMIT),
        name="sample_layer",
    )(x2, ktc, vtc, ktc, vtc, cpast, mktc, mvtc, w_in, w_kvt, w_out, u2, cw, cb, cg, cbeta, lg, lb)


def _prefix_matrix():
    j = lax.broadcasted_iota(jnp.int32, (TK, TK), 0)
    s = lax.broadcasted_iota(jnp.int32, (TK, TK), 1)
    return jnp.concatenate([(j > s).astype(BF16), jnp.ones((TK, TK), BF16)], axis=1)


def _time_minor(a):
    n, time, heads, dim = a.shape
    return jnp.transpose(a, (0, 2, 3, 1)).reshape(n, heads * dim, time)


def _time_major(a, heads):
    n, width, time = a.shape
    return jnp.transpose(a.reshape(n, heads, width // heads, time), (0, 3, 1, 2))[None]


def kernel(x_prompt, x_sample, cache_sb_k, cache_sb_v, cache_conv, cache_mem_k, cache_mem_v, mem_prompt, w_in, w_mem_kv, conv_w, conv_b, conv_ln_g, conv_ln_b, w_out, ln_g, ln_b):
    depth = w_in.shape[0]
    assert depth == 1, "single layer only"
    alpha = (2 * depth) ** 0.25
    S, n_new, _ = x_sample.shape

    u2 = _prefix_matrix()
    w = w_in[0]
    w_kv = w[:, REF_K:REF_V_END]
    w_in_b = jnp.concatenate([w[:, :REF_K], w[:, REF_V_END:], w_kv], axis=1).astype(BF16)
    w_kvt = w_kv.T.astype(BF16)
    w_memt = w_mem_kv[0].T.astype(BF16)
    w_out_b = w_out[0].astype(BF16)
    cw = conv_w[0]
    cb = conv_b[0].reshape(1, CONV_WIDTH)
    cg = conv_ln_g[0].reshape(1, CONV_WIDTH)
    cbeta = conv_ln_b[0].reshape(1, CONV_WIDTH)
    lg = ln_g[0].reshape(1, D_MODEL)
    lb = ln_b[0].reshape(1, D_MODEL)

    yp, ktp, vtp, cp, mktp, mvtp = _prompt_call(
        x_prompt, mem_prompt, w_in_b, w_kvt, w_memt, w_out_b, u2, cw, cb, cg, cbeta, lg, lb,
        alpha, tm=256)

    cpast = jnp.pad(cache_conv[0], ((0, 0), (HALO_PAD, 0), (0, 0)))
    ys, ks, vs, cs = _sample_call(
        x_sample, _time_minor(cache_sb_k[0]), _time_minor(cache_sb_v[0]), cpast,
        _time_minor(cache_mem_k[0]), _time_minor(cache_mem_v[0]),
        w_in_b, w_kvt, w_out_b, u2, cw, cb, cg, cbeta, lg, lb, alpha, win=256)

    return (
        yp,
        ys.reshape(S, n_new, D_MODEL),
        _time_major(ktp, SB_HEADS),
        _time_major(vtp, SB_HEADS),
        cp[None, :, HALO_PAD:, :],
        _time_major(mktp, MEM_HEADS),
        _time_major(mvtp, MEM_HEADS),
        ks.reshape(1, S, n_new, SB_HEADS, HEAD_DIM),
        vs.reshape(1, S, n_new, SB_HEADS, HEAD_DIM),
        cs[None, :, HALO_PAD:, :],
    )
```

```python
import functools
import math
from typing import Any, Callable, NamedTuple

import jax
import jax.numpy as jnp
from jax import lax
from jax.experimental import pallas as pl
from jax.experimental.pallas import tpu as pltpu

F32 = jnp.float32
BF16 = jnp.bfloat16

D_MODEL = 1024
SB_HEADS = 8
HEAD_DIM = 64
SB_WIDTH = SB_HEADS * HEAD_DIM
CONV_WIDTH = 256
CONV_K = 31
CONV_STATE = CONV_K - 1
MEM_HEADS = 4
MEM_WIDTH = MEM_HEADS * HEAD_DIM
N_MEM = 256
IN_WIDTH = 4 * SB_WIDTH + 3 * CONV_WIDTH + 2 * MEM_WIDTH
REF_K, REF_V_END = SB_WIDTH, 3 * SB_WIDTH
COL_Q = 0
COL_GA = SB_WIDTH
COL_CA = COL_GA + SB_WIDTH
COL_CB = COL_CA + CONV_WIDTH
COL_GC = COL_CB + CONV_WIDTH
COL_QM = COL_GC + CONV_WIDTH
COL_GM = COL_QM + MEM_WIDTH
H_WIDTH = COL_GM + MEM_WIDTH
COL_K = H_WIDTH
COL_V = COL_K + SB_WIDTH
COL_MIX_A, COL_MIX_C, COL_MIX_M = 0, SB_WIDTH, SB_WIDTH + CONV_WIDTH
LN_EPS = 1e-5
QK_SCALE = HEAD_DIM ** -0.5

LANES = 128
SUBLANES = 8
PAIRS = SB_HEADS // 2
TQ = 128
TK = 128
HALO = 32
HALO_PAD = HALO - CONV_STATE
CONV_ROWS = 64
PROJ_COLS = 256
SB_WAVE = 2
SB_SKEW = 2
WALK_STRIDE = 2
N_CHUNKS = H_WIDTH // PROJ_COLS
PROJ_SPLIT = 4
INV_LN2 = 1.0 / math.log(2.0)
QK_SCALE_LOG2 = QK_SCALE * INV_LN2
SKIP_LOG2 = -160.0
VMEM_LIMIT = 60 * 1024 * 1024

NT_DIMS = (((1,), (1,)), ((), ()))


def _sigmoid(x):
    return 1.0 / (1.0 + jnp.exp(-x))


def _silu(x):
    return x * _sigmoid(x)


def _project(xb, w_in_ref, h_ref, chunk):
    cols = slice(chunk * PROJ_COLS, (chunk + 1) * PROJ_COLS)
    h_ref[chunk] = jnp.dot(xb, w_in_ref[:, cols], preferred_element_type=F32)


def _hcols(h_ref, rows, col, width):
    off = col % PROJ_COLS
    assert off + width <= PROJ_COLS
    return h_ref[col // PROJ_COLS, rows, off:off + width]


def _lane_lo(n_rows):
    return lax.broadcasted_iota(jnp.int32, (n_rows, LANES), 1) < HEAD_DIM


def _build_qm(h_ref, rows, qm_ref):
    n = qm_ref.shape[1] // 2
    lane_lo = _lane_lo(n)
    for p in range(PAIRS):
        qp = _hcols(h_ref, rows, COL_Q + p * LANES, LANES) * QK_SCALE_LOG2
        qm_ref[p, 0:n] = jnp.where(lane_lo, qp, 0.0).astype(BF16)
        qm_ref[p, n:2 * n] = jnp.where(lane_lo, 0.0, qp).astype(BF16)


class _Part(NamedTuple):
    qm: Any
    acc: Any
    carry: Any
    get_kt: Callable[[int], Any]
    get_vt: Callable[[int], Any]
    mask: Any


def _sb_blocks(parts, u2_ref):
    chains = [(part, p) for part in parts for p in range(PAIRS)]
    n2 = parts[0].qm.shape[1]
    waves = [chains[i:i + SB_WAVE] for i in range(0, len(chains), SB_WAVE)]

    def scores(wave):
        return [jnp.dot(part.qm[p], part.get_kt(p), preferred_element_type=F32)
                for part, p in wave]

    def prefix_sums(wave, zs):
        hits, fails = [], []
        for (part, _), z in zip(wave, zs):
            nz = -z
            soft = jnp.log(1.0 + jnp.exp2(jnp.minimum(z, nz))) * INV_LN2
            log_fail = jnp.minimum(nz, 0.0) - soft
            hits.append(log_fail + z)
            if part.mask is not None:
                log_fail = jnp.where(part.mask, log_fail, 0.0)
            fails.append(log_fail.astype(BF16))
        sums = jnp.dot(jnp.concatenate(fails, axis=0), u2_ref[...],
                       preferred_element_type=F32)
        return hits, sums

    def accumulate(wave, hits, sums):
        for i, ((part, p), log_hit) in enumerate(zip(wave, hits)):
            s = sums[i * n2:(i + 1) * n2]
            carry = part.carry[p]
            w = jnp.exp2(log_hit + s[:, :TK] + carry)
            if part.mask is not None:
                w = jnp.where(part.mask, w, 0.0)
            part.carry[p] = carry + s[:, TK:]
            part.acc[p] += lax.dot_general(w.astype(BF16), part.get_vt(p), NT_DIMS,
                                           preferred_element_type=F32)

    zs, summed = {}, {}
    for k in range(len(waves) + 2 * SB_SKEW):
        if k < len(waves):
            zs[k] = scores(waves[k])
        if 0 <= k - SB_SKEW < len(waves):
            summed[k - SB_SKEW] = prefix_sums(waves[k - SB_SKEW], zs.pop(k - SB_SKEW))
        if 0 <= k - 2 * SB_SKEW < len(waves):
            accumulate(waves[k - 2 * SB_SKEW], *summed.pop(k - 2 * SB_SKEW))


def _carry_max(carry_refs):
    m = None
    for ref in carry_refs:
        for p in range(PAIRS):
            m = ref[p] if m is None else jnp.maximum(m, ref[p])
    return jnp.max(m)


def _sb_walk(first, n_blocks, m0, block_fn, carry_refs, stride=1):
    def cond(c):
        j, m = c
        return jnp.logical_and(j + stride <= n_blocks, m > SKIP_LOG2)

    def body(c):
        j, _ = c
        block_fn(j)
        return j + stride, _carry_max(carry_refs)

    return lax.while_loop(cond, body, (first, m0))


def _sb_finish(h_ref, rows, acc_ref, mix_ref):
    n = acc_ref.shape[1] // 2
    lane_lo = _lane_lo(n)
    for p in range(PAIRS):
        o = jnp.where(lane_lo, acc_ref[p, 0:n], acc_ref[p, n:2 * n])
        g = _hcols(h_ref, rows, COL_GA + p * LANES, LANES)
        mix_ref[rows, COL_MIX_A + p * LANES:COL_MIX_A + (p + 1) * LANES] = (
            o * _silu(g)).astype(BF16)


def _stacked_index(n_rows):
    row = lax.broadcasted_iota(jnp.int32, (2 * n_rows, TK), 0)
    col = lax.broadcasted_iota(jnp.int32, (2 * n_rows, TK), 1)
    return jnp.where(row >= n_rows, row - n_rows, row), col


def _conv_rows(h_ref, row, upad_ref, urow, nr, mix_ref, cw_ref, cb_ref, g_ref, b_ref):
    c = jnp.broadcast_to(cb_ref[...], (nr, CONV_WIDTH))
    for s in range(SUBLANES):
        taps = [i for i in range(CONV_K) if (HALO_PAD + i) % SUBLANES == s]
        span = nr if s == 0 else nr + SUBLANES
        part = None
        for i in taps:
            term = cw_ref[i:i + 1, :] * upad_ref[pl.ds(urow + (HALO_PAD + i - s), span), :]
            part = term if part is None else part + term
        c = c + part[s:s + nr]
    mu = jnp.mean(c, axis=-1, keepdims=True)
    d = c - mu
    var = jnp.mean(d * d, axis=-1, keepdims=True)
    n = d * lax.rsqrt(var + LN_EPS) * g_ref[...] + b_ref[...]
    gate = _hcols(h_ref, pl.ds(row, nr), COL_GC, CONV_WIDTH)
    mix_ref[pl.ds(row, nr), COL_MIX_C:COL_MIX_C + CONV_WIDTH] = (_silu(n) * _silu(gate)).astype(BF16)


def _mem_attend(h_ref, rows, n_rows, mkt, mvt, mix_ref):
    lane_lo = _lane_lo(n_rows)
    for p in range(MEM_HEADS // 2):
        qp = _hcols(h_ref, rows, COL_QM + p * LANES, LANES) * QK_SCALE
        mkp = mkt[p * LANES:(p + 1) * LANES, :]
        mvp = mvt[p * LANES:(p + 1) * LANES, :]
        outs = []
        for hh in range(2):
            qh = (jnp.where(lane_lo, qp, 0.0) if hh == 0 else jnp.where(lane_lo, 0.0, qp)).astype(BF16)
            s = jnp.dot(qh, mkp, preferred_element_type=F32)
            e = jnp.exp(s - jnp.max(s, axis=-1, keepdims=True))
            o = lax.dot_general(e.astype(BF16), mvp, NT_DIMS, preferred_element_type=F32)
            outs.append(o / jnp.sum(e, axis=-1, keepdims=True))
        o = jnp.where(lane_lo, outs[0], outs[1])
        g = _hcols(h_ref, rows, COL_GM + p * LANES, LANES)
        mix_ref[rows, COL_MIX_M + p * LANES:COL_MIX_M + (p + 1) * LANES] = (o * _silu(g)).astype(BF16)


def _out_norm(x, mix_ref, w_out_ref, g_ref, b_ref, alpha):
    r = alpha * x + jnp.dot(mix_ref[...], w_out_ref[...], preferred_element_type=F32)
    mu = jnp.mean(r, axis=-1, keepdims=True)
    d = r - mu
    var = jnp.mean(d * d, axis=-1, keepdims=True)
    return d * lax.rsqrt(var + LN_EPS) * g_ref[...] + b_ref[...]


def _prompt_kernel(x_ref, xn_ref, mem_ref, w_in_ref, w_kvt_ref, w_memt_ref, w_out_ref, u2_ref, cw_ref, cb_ref,
                   cg_ref, cbeta_ref, lg_ref, lb_ref,
                   y_ref, kt_ref, vt_ref, conv_ref, mkt_ref, mvt_ref,
                   h_scr, kt_scr, vt_scr, mkt_scr, mvt_scr, qm_scr, acc_scr, carry_scr,
                   upad_scr, mix_scr, *, tm, alpha):
    t = pl.program_id(1)
    n_sub = tm // TQ
    assert tm % CONV_ROWS == 0
    row, col = _stacked_index(TQ)
    causal = col < row

    @pl.when(t == 0)
    def _():
        kvt = lax.dot_general(w_memt_ref[...], mem_ref[0].astype(BF16), NT_DIMS,
                              preferred_element_type=F32)
        mkt_ref[0] = kvt[:MEM_WIDTH]
        mvt_ref[0] = kvt[MEM_WIDTH:]
        mkt_scr[...] = kvt[:MEM_WIDTH].astype(BF16)
        mvt_scr[...] = kvt[MEM_WIDTH:].astype(BF16)
        upad_scr[0:HALO, :] = jnp.zeros((HALO, CONV_WIDTH), F32)

    h_cur = h_scr.at[t % 2]
    h_next = h_scr.at[(t + 1) % 2]
    xb = x_ref[0].astype(BF16)

    @pl.when(t == 0)
    def _():
        for c in range(N_CHUNKS):
            _project(xb, w_in_ref, h_scr.at[0], c)

    kvt = lax.dot_general(w_kvt_ref[...], xb, NT_DIMS, preferred_element_type=F32)
    kt_ref[0] = kvt[:SB_WIDTH]
    vt_ref[0] = kvt[SB_WIDTH:]
    blk0 = t * n_sub
    for c in range(n_sub):
        kt_scr[blk0 + c] = kvt[:SB_WIDTH, c * TK:(c + 1) * TK].astype(BF16)
        vt_scr[blk0 + c] = kvt[SB_WIDTH:, c * TK:(c + 1) * TK].astype(BF16)

    for i in range(n_sub):
        _build_qm(h_cur, slice(i * TQ, (i + 1) * TQ), qm_scr.at[i])
    acc_scr[...] = jnp.zeros(acc_scr.shape, F32)
    carry_scr[...] = jnp.zeros(carry_scr.shape, F32)
    carries = [carry_scr.at[i] for i in range(n_sub)]

    def part(i, blk, mask):
        return _Part(qm_scr.at[i], acc_scr.at[i], carry_scr.at[i],
                     lambda p: kt_scr[blk, p * LANES:(p + 1) * LANES, :],
                     lambda p: vt_scr[blk, p * LANES:(p + 1) * LANES, :], mask)

    xb_next = xn_ref[0].astype(BF16)
    for c in range(PROJ_SPLIT):
        _project(xb_next, w_in_ref, h_next, c)
    _sb_blocks([part(i, blk0 + i, causal) for i in range(n_sub)], u2_ref)
    def walk(j, stride):
        _sb_blocks([part(i, blk0 + i - 1 - j - k, None)
                    for k in range(stride) for i in range(n_sub)], u2_ref)

    j, m = _sb_walk(jnp.int32(0), blk0, jnp.float32(0.0),
                    functools.partial(walk, stride=WALK_STRIDE), carries, WALK_STRIDE)
    j, m = _sb_walk(j, blk0, m, functools.partial(walk, stride=1), carries)
    for d in range(n_sub - 1):
        def tail(d=d):
            _sb_blocks([part(i, i - 1 - d, None) for i in range(d + 1, n_sub)], u2_ref)
            return _carry_max(carries)
        m = lax.cond(jnp.logical_and(j >= blk0, m > SKIP_LOG2), tail, lambda m=m: m)
    for i in range(n_sub):
        _sb_finish(h_cur, slice(i * TQ, (i + 1) * TQ), acc_scr.at[i], mix_scr)

    for c in range(PROJ_SPLIT, N_CHUNKS):
        _project(xb_next, w_in_ref, h_next, c)
    u = (_hcols(h_cur, slice(0, tm), COL_CA, CONV_WIDTH)
         * _sigmoid(_hcols(h_cur, slice(0, tm), COL_CB, CONV_WIDTH)))
    upad_scr[HALO:HALO + tm, :] = u

    for r in range(0, tm, CONV_ROWS):
        _conv_rows(h_cur, r, upad_scr, r, CONV_ROWS, mix_scr, cw_ref, cb_ref, cg_ref, cbeta_ref)
    last_rows = upad_scr[tm:tm + HALO, :]
    upad_scr[0:HALO, :] = last_rows
    conv_ref[0] = last_rows

    _mem_attend(h_cur, slice(0, tm), tm, mkt_scr[...], mvt_scr[...], mix_scr)
    y_ref[0] = _out_norm(x_ref[0], mix_scr, w_out_ref, lg_ref, lb_ref, alpha)


def _const_spec(shape):
    return pl.BlockSpec(shape, lambda *_: (0,) * len(shape), pipeline_mode=pl.Buffered(1))


def _prompt_call(x, mem, w_in, w_kvt, w_memt, w_out, u2, cw, cb, cg, cbeta, lg, lb, alpha, tm):
    B, T, _ = x.shape
    assert T % tm == 0 and tm % TQ == 0 and TQ == TK
    grid = (B, T // tm)
    row_spec = lambda width: pl.BlockSpec((1, tm, width), lambda b, t: (b, t, 0))
    col_spec = lambda height: pl.BlockSpec((1, height, tm), lambda b, t: (b, 0, t))
    per_b = lambda r, width: pl.BlockSpec((1, r, width), lambda b, t: (b, 0, 0))
    out_shape = (
        jax.ShapeDtypeStruct((B, T, D_MODEL), F32),
        jax.ShapeDtypeStruct((B, SB_WIDTH, T), F32),
        jax.ShapeDtypeStruct((B, SB_WIDTH, T), F32),
        jax.ShapeDtypeStruct((B, HALO, CONV_WIDTH), F32),
        jax.ShapeDtypeStruct((B, MEM_WIDTH, N_MEM), F32),
        jax.ShapeDtypeStruct((B, MEM_WIDTH, N_MEM), F32),
    )
    scratch = [
        pltpu.VMEM((2, N_CHUNKS, tm, PROJ_COLS), F32),
        pltpu.VMEM((T // TK, SB_WIDTH, TK), BF16),
        pltpu.VMEM((T // TK, SB_WIDTH, TK), BF16),
        pltpu.VMEM((MEM_WIDTH, N_MEM), BF16),
        pltpu.VMEM((MEM_WIDTH, N_MEM), BF16),
        pltpu.VMEM((tm // TQ, PAIRS, 2 * TQ, LANES), BF16),
        pltpu.VMEM((tm // TQ, PAIRS, 2 * TQ, LANES), F32),
        pltpu.VMEM((tm // TQ, PAIRS, 2 * TQ, LANES), F32),
        pltpu.VMEM((HALO + tm, CONV_WIDTH), F32),
        pltpu.VMEM((tm, D_MODEL), BF16),
    ]
    return pl.pallas_call(
        functools.partial(_prompt_kernel, tm=tm, alpha=alpha),
        grid=grid,
        in_specs=[
            row_spec(D_MODEL),
            pl.BlockSpec((1, tm, D_MODEL), lambda b, t: (b, jnp.minimum(t + 1, T // tm - 1), 0)),
            per_b(N_MEM, D_MODEL),
            _const_spec((D_MODEL, H_WIDTH)),
            _const_spec((2 * SB_WIDTH, D_MODEL)),
            _const_spec((2 * MEM_WIDTH, D_MODEL)),
            _const_spec((D_MODEL, D_MODEL)),
            _const_spec((TK, 2 * TK)),
            _const_spec((CONV_K, CONV_WIDTH)),
            _const_spec((1, CONV_WIDTH)),
            _const_spec((1, CONV_WIDTH)),
            _const_spec((1, CONV_WIDTH)),
            _const_spec((1, D_MODEL)),
            _const_spec((1, D_MODEL)),
        ],
        out_specs=(
            row_spec(D_MODEL), col_spec(SB_WIDTH), col_spec(SB_WIDTH),
            per_b(HALO, CONV_WIDTH), per_b(MEM_WIDTH, N_MEM), per_b(MEM_WIDTH, N_MEM),
        ),
        out_shape=out_shape,
        scratch_shapes=scratch,
        compiler_params=pltpu.CompilerParams(
            dimension_semantics=("arbitrary", "arbitrary"),
            vmem_limit_bytes=VMEM_LIMIT),
        name="prompt_layer",
    )(x, x, mem, w_in, w_kvt, w_memt, w_out, u2, cw, cb, cg, cbeta, lg, lb)


def _sample_kernel(x_ref, ktwin_ref, vtwin_ref, ktc_hbm, vtc_hbm, cpast_ref, mktc_ref, mvtc_ref,
                   w_in_ref, w_kvt_ref, w_out_ref, u2_ref, cw_ref, cb_ref, cg_ref, cbeta_ref,
                   lg_ref, lb_ref,
                   y_ref, k_ref, v_ref, conv_ref,
                   h_scr, ktnew_scr, vtnew_scr, ktbuf, vtbuf, sem, qm_scr, acc_scr, carry_scr,
                   upad_scr, mix_scr, *, group, n_new, win, past, alpha):
    step = pl.program_id(0)
    row, col = _stacked_index(n_new)
    n_win = win // TK
    n_blocks = past // TK

    xb = x_ref[...].astype(BF16)
    for c in range(IN_WIDTH // PROJ_COLS):
        _project(xb, w_in_ref, h_scr, c)
    for c in range(SB_WIDTH // PROJ_COLS):
        cols = slice(c * PROJ_COLS, (c + 1) * PROJ_COLS)
        k_ref[:, cols] = h_scr[COL_K // PROJ_COLS + c]
        v_ref[:, cols] = h_scr[COL_V // PROJ_COLS + c]
    kvt = lax.dot_general(w_kvt_ref[...], xb, NT_DIMS, preferred_element_type=F32)
    ktnew_scr[...] = kvt[:SB_WIDTH].astype(BF16)
    vtnew_scr[...] = kvt[SB_WIDTH:].astype(BF16)

    for g in range(group):
        _build_qm(h_scr, slice(g * n_new, (g + 1) * n_new), qm_scr.at[g])
    acc_scr[...] = jnp.zeros(acc_scr.shape, F32)
    carry_scr[...] = jnp.zeros(carry_scr.shape, F32)
    carries = [carry_scr.at[g] for g in range(group)]

    def part(g, get_kt, get_vt, mask):
        return _Part(qm_scr.at[g], acc_scr.at[g], carry_scr.at[g], get_kt, get_vt, mask)

    def own_past(g):
        rel = col - g * n_new
        return jnp.logical_and(rel >= 0, rel < row)

    _sb_blocks([part(g, lambda p: ktnew_scr[p * LANES:(p + 1) * LANES, :],
                     lambda p: vtnew_scr[p * LANES:(p + 1) * LANES, :], own_past(g))
                for g in range(group)], u2_ref)

    def win_blocks(j):
        cols = slice(win - (j + 1) * TK, win - j * TK)
        _sb_blocks([part(g, lambda p, g=g: ktwin_ref[g, p * LANES:(p + 1) * LANES, cols].astype(BF16),
                         lambda p, g=g: vtwin_ref[g, p * LANES:(p + 1) * LANES, cols].astype(BF16),
                         None)
                    for g in range(group)], u2_ref)
        return _carry_max(carries)

    m = jnp.float32(0.0)
    for j in range(n_win):
        m = lax.cond(m > SKIP_LOG2, functools.partial(win_blocks, j), lambda m=m: m)

    def far_walk(g, _):
        stream = step * group + g

        def far_block(j):
            start = pl.multiple_of(past - (j + 1) * TK, TK)
            ck = pltpu.make_async_copy(ktc_hbm.at[stream, :, pl.ds(start, TK)], ktbuf, sem.at[0])
            cv = pltpu.make_async_copy(vtc_hbm.at[stream, :, pl.ds(start, TK)], vtbuf, sem.at[1])
            ck.start()
            cv.start()
            ck.wait()
            cv.wait()
            _sb_blocks([part(g, lambda p: ktbuf[p * LANES:(p + 1) * LANES, :].astype(BF16),
                             lambda p: vtbuf[p * LANES:(p + 1) * LANES, :].astype(BF16), None)],
                       u2_ref)

        own = [carry_scr.at[g]]
        _sb_walk(jnp.int32(n_win), n_blocks, _carry_max(own), far_block, own)
        return 0

    lax.fori_loop(0, group, far_walk, 0)
    for g in range(group):
        rows = slice(g * n_new, (g + 1) * n_new)
        _sb_finish(h_scr, rows, acc_scr.at[g], mix_scr)
        _mem_attend(h_scr, rows, n_new, mktc_ref[g].astype(BF16), mvtc_ref[g].astype(BF16), mix_scr)

    u = (_hcols(h_scr, slice(0, group * n_new), COL_CA, CONV_WIDTH)
         * _sigmoid(_hcols(h_scr, slice(0, group * n_new), COL_CB, CONV_WIDTH)))
    for g in range(group):
        base = g * (HALO + n_new)
        upad_scr[base:base + HALO, :] = cpast_ref[g]
        upad_scr[base + HALO:base + HALO + n_new, :] = u[g * n_new:(g + 1) * n_new]
        _conv_rows(h_scr, g * n_new, upad_scr, base, n_new, mix_scr, cw_ref, cb_ref, cg_ref, cbeta_ref)
        conv_ref[g] = upad_scr[base + n_new:base + n_new + HALO, :]

    y_ref[...] = _out_norm(x_ref[...], mix_scr, w_out_ref, lg_ref, lb_ref, alpha)


def _sample_call(x, ktc, vtc, cpast, mktc, mvtc, w_in, w_kvt, w_out, u2, cw, cb, cg, cbeta, lg, lb,
                 alpha, win):
    S, n_new, _ = x.shape
    past = ktc.shape[2]
    group = TK // n_new
    assert group * n_new == TK and S % group == 0 and past % win == 0 and win % TK == 0
    assert n_new % 16 == 0 and n_new >= CONV_STATE
    rows = group * n_new
    x2 = x.reshape(S * n_new, D_MODEL)
    row_spec = lambda width: pl.BlockSpec((rows, width), lambda i: (i, 0))
    grp = lambda r, width: pl.BlockSpec((group, r, width), lambda i: (i, 0, 0))
    win_spec = pl.BlockSpec((group, SB_WIDTH, win), lambda i: (i, 0, past // win - 1))
    any_spec = pl.BlockSpec(memory_space=pl.ANY)
    out_shape = (
        jax.ShapeDtypeStruct((S * n_new, D_MODEL), F32),
        jax.ShapeDtypeStruct((S * n_new, SB_WIDTH), F32),
        jax.ShapeDtypeStruct((S * n_new, SB_WIDTH), F32),
        jax.ShapeDtypeStruct((S, HALO, CONV_WIDTH), F32),
    )
    scratch = [
        pltpu.VMEM((IN_WIDTH // PROJ_COLS, rows, PROJ_COLS), F32),
        pltpu.VMEM((SB_WIDTH, TK), BF16),
        pltpu.VMEM((SB_WIDTH, TK), BF16),
        pltpu.VMEM((SB_WIDTH, TK), F32),
        pltpu.VMEM((SB_WIDTH, TK), F32),
        pltpu.SemaphoreType.DMA((2,)),
        pltpu.VMEM((group, PAIRS, 2 * n_new, LANES), BF16),
        pltpu.VMEM((group, PAIRS, 2 * n_new, LANES), F32),
        pltpu.VMEM((group, PAIRS, 2 * n_new, LANES), F32),
        pltpu.VMEM((group * (HALO + n_new), CONV_WIDTH), F32),
        pltpu.VMEM((rows, D_MODEL), BF16),
    ]
    return pl.pallas_call(
        functools.partial(_sample_kernel, group=group, n_new=n_new, win=win, past=past, alpha=alpha),
        grid=(S // group,),
        in_specs=[
            row_spec(D_MODEL), win_spec, win_spec, any_spec, any_spec,
            grp(HALO, CONV_WIDTH), grp(MEM_WIDTH, N_MEM), grp(MEM_WIDTH, N_MEM),
            _const_spec((D_MODEL, IN_WIDTH)),
            _const_spec((2 * SB_WIDTH, D_MODEL)),
            _const_spec((D_MODEL, D_MODEL)),
            _const_spec((TK, 2 * TK)),
            _const_spec((CONV_K, CONV_WIDTH)),
            _const_spec((1, CONV_WIDTH)),
            _const_spec((1, CONV_WIDTH)),
            _const_spec((1, CONV_WIDTH)),
            _const_spec((1, D_MODEL)),
            _const_spec((1, D_MODEL)),
        ],
        out_specs=(row_spec(D_MODEL), row_spec(SB_WIDTH), row_spec(SB_WIDTH), grp(HALO, CONV_WIDTH)),
        out_shape=out_shape,
        scratch_shapes=scratch,
        compiler_params=pltpu.CompilerParams(
            dimension_semantics=("arbitrary",),
            vmem_limit_bytes=VMEM_LIMIT),
        name="sample_layer",
    )(x2, ktc, vtc, ktc, vtc, cpast, mktc, mvtc, w_in, w_kvt, w_out, u2, cw, cb, cg, cbeta, lg, lb)


def _prefix_matrix():
    j = lax.broadcasted_iota(jnp.int32, (TK, TK), 0)
    s = lax.broadcasted_iota(jnp.int32, (TK, TK), 1)
    return jnp.concatenate([(j > s).astype(BF16), jnp.ones((TK, TK), BF16)], axis=1)


def _time_minor(a):
    n, time, heads, dim = a.shape
    return jnp.transpose(a, (0, 2, 3, 1)).reshape(n, heads * dim, time)


def _time_major(a, heads):
    n, width, time = a.shape
    return jnp.transpose(a.reshape(n, heads, width // heads, time), (0, 3, 1, 2))[None]


def kernel(x_prompt, x_sample, cache_sb_k, cache_sb_v, cache_conv, cache_mem_k, cache_mem_v, mem_prompt, w_in, w_mem_kv, conv_w, conv_b, conv_ln_g, conv_ln_b, w_out, ln_g, ln_b):
    depth = w_in.shape[0]
    assert depth == 1, "single layer only"
    alpha = (2 * depth) ** 0.25
    S, n_new, _ = x_sample.shape

    u2 = _prefix_matrix()
    w = w_in[0]
    w_kv = w[:, REF_K:REF_V_END]
    w_in_b = jnp.concatenate([w[:, :REF_K], w[:, REF_V_END:], w_kv], axis=1).astype(BF16)
    w_kvt = w_kv.T.astype(BF16)
    w_memt = w_mem_kv[0].T.astype(BF16)
    w_out_b = w_out[0].astype(BF16)
    cw = conv_w[0]
    cb = conv_b[0].reshape(1, CONV_WIDTH)
    cg = conv_ln_g[0].reshape(1, CONV_WIDTH)
    cbeta = conv_ln_b[0].reshape(1, CONV_WIDTH)
    lg = ln_g[0].reshape(1, D_MODEL)
    lb = ln_b[0].reshape(1, D_MODEL)

    yp, ktp, vtp, cp, mktp, mvtp = _prompt_call(
        x_prompt, mem_prompt, w_in_b, w_kvt, w_memt, w_out_b, u2, cw, cb, cg, cbeta, lg, lb,
        alpha, tm=256)

    cpast = jnp.pad(cache_conv[0], ((0, 0), (HALO_PAD, 0), (0, 0)))
    ys, ks, vs, cs = _sample_call(
        x_sample, _time_minor(cache_sb_k[0]), _time_minor(cache_sb_v[0]), cpast,
        _time_minor(cache_mem_k[0]), _time_minor(cache_mem_v[0]),
        w_in_b, w_kvt, w_out_b, u2, cw, cb, cg, cbeta, lg, lb, alpha, win=256)

    return (
        yp,
        ys.reshape(S, n_new, D_MODEL),
        _time_major(ktp, SB_HEADS),
        _time_major(vtp, SB_HEADS),
        cp[None, :, HALO_PAD:, :],
        _time_major(mktp, MEM_HEADS),
        _time_major(mvtp, MEM_HEADS),
        ks.reshape(1, S, n_new, SB_HEADS, HEAD_DIM),
        vs.reshape(1, S, n_new, SB_HEADS, HEAD_DIM),
        cs[None, :, HALO_PAD:, :],
    )
```

```python
import functools
import math
from typing import Any, Callable, NamedTuple

import jax
import jax.numpy as jnp
from jax import lax
from jax.experimental import pallas as pl
from jax.experimental.pallas import tpu as pltpu

F32 = jnp.float32
BF16 = jnp.bfloat16

D_MODEL = 1024
SB_HEADS = 8
HEAD_DIM = 64
SB_WIDTH = SB_HEADS * HEAD_DIM
CONV_WIDTH = 256
CONV_K = 31
CONV_STATE = CONV_K - 1
MEM_HEADS = 4
MEM_WIDTH = MEM_HEADS * HEAD_DIM
N_MEM = 256
IN_WIDTH = 4 * SB_WIDTH + 3 * CONV_WIDTH + 2 * MEM_WIDTH
REF_K, REF_V_END = SB_WIDTH, 3 * SB_WIDTH
COL_Q = 0
COL_GA = SB_WIDTH
COL_CA = COL_GA + SB_WIDTH
COL_CB = COL_CA + CONV_WIDTH
COL_GC = COL_CB + CONV_WIDTH
COL_QM = COL_GC + CONV_WIDTH
COL_GM = COL_QM + MEM_WIDTH
H_WIDTH = COL_GM + MEM_WIDTH
COL_K = H_WIDTH
COL_V = COL_K + SB_WIDTH
COL_MIX_A, COL_MIX_C, COL_MIX_M = 0, SB_WIDTH, SB_WIDTH + CONV_WIDTH
LN_EPS = 1e-5
QK_SCALE = HEAD_DIM ** -0.5

LANES = 128
SUBLANES = 8
PAIRS = SB_HEADS // 2
TQ = 128
TK = 128
HALO = 32
HALO_PAD = HALO - CONV_STATE
CONV_ROWS = 64
PROJ_COLS = 256
SB_WAVE = 2
SB_SKEW = 2
TOP_ROWS = 32
N_CHUNKS = H_WIDTH // PROJ_COLS
PROJ_SPLIT = 4
INV_LN2 = 1.0 / math.log(2.0)
QK_SCALE_LOG2 = QK_SCALE * INV_LN2
SKIP_LOG2 = -152.0
VMEM_LIMIT = 60 * 1024 * 1024

NT_DIMS = (((1,), (1,)), ((), ()))


def _sigmoid(x):
    return 1.0 / (1.0 + jnp.exp(-x))


def _silu(x):
    return x * _sigmoid(x)


def _project(xb, w_in_ref, h_ref, chunk):
    cols = slice(chunk * PROJ_COLS, (chunk + 1) * PROJ_COLS)
    h_ref[chunk] = jnp.dot(xb, w_in_ref[:, cols], preferred_element_type=F32)


def _hcols(h_ref, rows, col, width):
    off = col % PROJ_COLS
    assert off + width <= PROJ_COLS
    return h_ref[col // PROJ_COLS, rows, off:off + width]


def _lane_lo(n_rows):
    return lax.broadcasted_iota(jnp.int32, (n_rows, LANES), 1) < HEAD_DIM


def _build_qm(h_ref, rows, qm_ref):
    n = qm_ref.shape[1] // 2
    lane_lo = _lane_lo(n)
    for p in range(PAIRS):
        qp = _hcols(h_ref, rows, COL_Q + p * LANES, LANES) * QK_SCALE_LOG2
        qm_ref[p, 0:n] = jnp.where(lane_lo, qp, 0.0).astype(BF16)
        qm_ref[p, n:2 * n] = jnp.where(lane_lo, 0.0, qp).astype(BF16)


class _Part(NamedTuple):
    qm: Callable[[int], Any]
    carry_get: Callable[[int], Any]
    carry_set: Callable[[int, Any], None]
    acc_add: Callable[[int, Any], None]
    get_kt: Callable[[int], Any]
    get_vt: Callable[[int], Any]
    mask: Any


def _whole_part(qm_ref, acc_ref, carry_ref, get_kt, get_vt, mask):
    def carry_set(p, v):
        carry_ref[p] = v

    def acc_add(p, v):
        acc_ref[p] += v

    return _Part(lambda p: qm_ref[p], lambda p: carry_ref[p], carry_set, acc_add,
                 get_kt, get_vt, mask)


def _rows_part(qm_ref, acc_ref, carry_ref, get_kt, get_vt, lo, hi):
    n = qm_ref.shape[1] // 2
    k = hi - lo
    head_a, head_b = slice(lo, hi), slice(n + lo, n + hi)

    def gather(ref, p):
        return jnp.concatenate([ref[p, head_a], ref[p, head_b]], axis=0)

    def carry_set(p, v):
        carry_ref[p, head_a] = v[:k]
        carry_ref[p, head_b] = v[k:]

    def acc_add(p, v):
        acc_ref[p, head_a] += v[:k]
        acc_ref[p, head_b] += v[k:]

    return _Part(lambda p: gather(qm_ref, p), lambda p: gather(carry_ref, p), carry_set, acc_add,
                 get_kt, get_vt, None)


def _sb_blocks(parts, u2_ref):
    chains = [(part, p) for part in parts for p in range(PAIRS)]
    waves = [chains[i:i + SB_WAVE] for i in range(0, len(chains), SB_WAVE)]

    def scores(wave):
        return [jnp.dot(part.qm(p), part.get_kt(p), preferred_element_type=F32)
                for part, p in wave]

    def prefix_sums(wave, zs):
        hits, fails = [], []
        for (part, _), z in zip(wave, zs):
            nz = -z
            soft = jnp.log(1.0 + jnp.exp2(jnp.minimum(z, nz))) * INV_LN2
            log_fail = jnp.minimum(nz, 0.0) - soft
            hits.append(log_fail + z)
            if part.mask is not None:
                log_fail = jnp.where(part.mask, log_fail, 0.0)
            fails.append(log_fail.astype(BF16))
        sums = jnp.dot(jnp.concatenate(fails, axis=0), u2_ref[...],
                       preferred_element_type=F32)
        return hits, sums

    def accumulate(wave, hits, sums):
        row = 0
        for (part, p), log_hit in zip(wave, hits):
            s = sums[row:row + log_hit.shape[0]]
            row += log_hit.shape[0]
            carry = part.carry_get(p)
            w = jnp.exp2(log_hit + s[:, :TK] + carry)
            if part.mask is not None:
                w = jnp.where(part.mask, w, 0.0)
            part.carry_set(p, carry + s[:, TK:])
            part.acc_add(p, lax.dot_general(w.astype(BF16), part.get_vt(p), NT_DIMS,
                                            preferred_element_type=F32))

    zs, summed = {}, {}
    for k in range(len(waves) + 2 * SB_SKEW):
        if k < len(waves):
            zs[k] = scores(waves[k])
        if 0 <= k - SB_SKEW < len(waves):
            summed[k - SB_SKEW] = prefix_sums(waves[k - SB_SKEW], zs.pop(k - SB_SKEW))
        if 0 <= k - 2 * SB_SKEW < len(waves):
            accumulate(waves[k - 2 * SB_SKEW], *summed.pop(k - 2 * SB_SKEW))


def _carry_max(carry_refs, lo=0, hi=None):
    n = carry_refs[0].shape[1] // 2
    hi = n if hi is None else hi
    m = None
    for ref in carry_refs:
        for p in range(PAIRS):
            for rows in (slice(lo, hi), slice(n + lo, n + hi)):
                m = ref[p, rows] if m is None else jnp.maximum(m, ref[p, rows])
    return jnp.max(m)


def _sb_walk(first, n_blocks, m0, block_fn, carry_refs):
    def cond(c):
        j, m = c
        return jnp.logical_and(j < n_blocks, m > SKIP_LOG2)

    def body(c):
        j, _ = c
        block_fn(j)
        return j + 1, _carry_max(carry_refs)

    return lax.while_loop(cond, body, (first, m0))


def _sb_finish(h_ref, rows, acc_ref, mix_ref):
    n = acc_ref.shape[1] // 2
    lane_lo = _lane_lo(n)
    for p in range(PAIRS):
        o = jnp.where(lane_lo, acc_ref[p, 0:n], acc_ref[p, n:2 * n])
        g = _hcols(h_ref, rows, COL_GA + p * LANES, LANES)
        mix_ref[rows, COL_MIX_A + p * LANES:COL_MIX_A + (p + 1) * LANES] = (
            o * _silu(g)).astype(BF16)


def _stacked_index(n_rows):
    row = lax.broadcasted_iota(jnp.int32, (2 * n_rows, TK), 0)
    col = lax.broadcasted_iota(jnp.int32, (2 * n_rows, TK), 1)
    return jnp.where(row >= n_rows, row - n_rows, row), col


def _conv_rows(h_ref, row, upad_ref, urow, nr, mix_ref, cw_ref, cb_ref, g_ref, b_ref):
    c = jnp.broadcast_to(cb_ref[...], (nr, CONV_WIDTH))
    for s in range(SUBLANES):
        taps = [i for i in range(CONV_K) if (HALO_PAD + i) % SUBLANES == s]
        span = nr if s == 0 else nr + SUBLANES
        part = None
        for i in taps:
            term = cw_ref[i:i + 1, :] * upad_ref[pl.ds(urow + (HALO_PAD + i - s), span), :]
            part = term if part is None else part + term
        c = c + part[s:s + nr]
    mu = jnp.mean(c, axis=-1, keepdims=True)
    d = c - mu
    var = jnp.mean(d * d, axis=-1, keepdims=True)
    n = d * lax.rsqrt(var + LN_EPS) * g_ref[...] + b_ref[...]
    gate = _hcols(h_ref, pl.ds(row, nr), COL_GC, CONV_WIDTH)
    mix_ref[pl.ds(row, nr), COL_MIX_C:COL_MIX_C + CONV_WIDTH] = (_silu(n) * _silu(gate)).astype(BF16)


def _mem_attend(h_ref, rows, n_rows, mkt, mvt, mix_ref):
    lane_lo = _lane_lo(n_rows)
    for p in range(MEM_HEADS // 2):
        qp = _hcols(h_ref, rows, COL_QM + p * LANES, LANES) * QK_SCALE
        mkp = mkt[p * LANES:(p + 1) * LANES, :]
        mvp = mvt[p * LANES:(p + 1) * LANES, :]
        outs = []
        for hh in range(2):
            qh = (jnp.where(lane_lo, qp, 0.0) if hh == 0 else jnp.where(lane_lo, 0.0, qp)).astype(BF16)
            s = jnp.dot(qh, mkp, preferred_element_type=F32)
            e = jnp.exp(s - jnp.max(s, axis=-1, keepdims=True))
            o = lax.dot_general(e.astype(BF16), mvp, NT_DIMS, preferred_element_type=F32)
            outs.append(o / jnp.sum(e, axis=-1, keepdims=True))
        o = jnp.where(lane_lo, outs[0], outs[1])
        g = _hcols(h_ref, rows, COL_GM + p * LANES, LANES)
        mix_ref[rows, COL_MIX_M + p * LANES:COL_MIX_M + (p + 1) * LANES] = (o * _silu(g)).astype(BF16)


def _out_norm(x, mix_ref, w_out_ref, g_ref, b_ref, alpha):
    r = alpha * x + jnp.dot(mix_ref[...], w_out_ref[...], preferred_element_type=F32)
    mu = jnp.mean(r, axis=-1, keepdims=True)
    d = r - mu
    var = jnp.mean(d * d, axis=-1, keepdims=True)
    return d * lax.rsqrt(var + LN_EPS) * g_ref[...] + b_ref[...]


def _prompt_kernel(x_ref, xn_ref, mem_ref, w_in_ref, w_kvt_ref, w_memt_ref, w_out_ref, u2_ref, cw_ref, cb_ref,
                   cg_ref, cbeta_ref, lg_ref, lb_ref,
                   y_ref, kt_ref, vt_ref, conv_ref, mkt_ref, mvt_ref,
                   h_scr, kt_scr, vt_scr, mkt_scr, mvt_scr, qm_scr, acc_scr, carry_scr,
                   upad_scr, mix_scr, *, tm, alpha):
    t = pl.program_id(1)
    n_sub = tm // TQ
    assert tm % CONV_ROWS == 0
    row, col = _stacked_index(TQ)
    causal = col < row

    @pl.when(t == 0)
    def _():
        kvt = lax.dot_general(w_memt_ref[...], mem_ref[0].astype(BF16), NT_DIMS,
                              preferred_element_type=F32)
        mkt_ref[0] = kvt[:MEM_WIDTH]
        mvt_ref[0] = kvt[MEM_WIDTH:]
        mkt_scr[...] = kvt[:MEM_WIDTH].astype(BF16)
        mvt_scr[...] = kvt[MEM_WIDTH:].astype(BF16)
        upad_scr[0:HALO, :] = jnp.zeros((HALO, CONV_WIDTH), F32)

    h_cur = h_scr.at[t % 2]
    h_next = h_scr.at[(t + 1) % 2]
    xb = x_ref[0].astype(BF16)

    @pl.when(t == 0)
    def _():
        for c in range(N_CHUNKS):
            _project(xb, w_in_ref, h_scr.at[0], c)

    kvt = lax.dot_general(w_kvt_ref[...], xb, NT_DIMS, preferred_element_type=F32)
    kt_ref[0] = kvt[:SB_WIDTH]
    vt_ref[0] = kvt[SB_WIDTH:]
    blk0 = t * n_sub
    for c in range(n_sub):
        kt_scr[blk0 + c] = kvt[:SB_WIDTH, c * TK:(c + 1) * TK].astype(BF16)
        vt_scr[blk0 + c] = kvt[SB_WIDTH:, c * TK:(c + 1) * TK].astype(BF16)

    for i in range(n_sub):
        _build_qm(h_cur, slice(i * TQ, (i + 1) * TQ), qm_scr.at[i])
    acc_scr[...] = jnp.zeros(acc_scr.shape, F32)
    carry_scr[...] = jnp.zeros(carry_scr.shape, F32)
    carries = [carry_scr.at[i] for i in range(n_sub)]

    def refs(i, blk):
        return (qm_scr.at[i], acc_scr.at[i], carry_scr.at[i],
                lambda p: kt_scr[blk, p * LANES:(p + 1) * LANES, :],
                lambda p: vt_scr[blk, p * LANES:(p + 1) * LANES, :])

    def whole(j, subs=range(n_sub)):
        return [_whole_part(*refs(i, blk0 + i - 1 - j), None) for i in subs]

    def some_rows(j, lo, hi):
        return [_rows_part(*refs(i, blk0 + i - 1 - j), lo, hi) for i in range(n_sub)]

    def left():
        return _carry_max(carries, 0, TOP_ROWS), _carry_max(carries, TOP_ROWS, TQ)

    xb_next = xn_ref[0].astype(BF16)
    for c in range(PROJ_SPLIT):
        _project(xb_next, w_in_ref, h_next, c)
    _sb_blocks([_whole_part(*refs(i, blk0 + i), causal) for i in range(n_sub)], u2_ref)

    def speculate():
        _sb_blocks(whole(0) + some_rows(1, 0, TOP_ROWS), u2_ref)
        return left()

    def all_left():
        return jnp.float32(0.0), jnp.float32(0.0)

    ahead = blk0 >= 2
    m_top, m_bot = lax.cond(ahead, speculate, all_left)

    def catch_up():
        _sb_blocks(some_rows(1, TOP_ROWS, TQ), u2_ref)
        return _carry_max(carries, TOP_ROWS, TQ)

    m_bot = lax.cond(jnp.logical_and(ahead, m_bot > SKIP_LOG2), catch_up, lambda: m_bot)

    def trip_cond(c):
        j, top, bot = c
        return jnp.logical_and(j < blk0, jnp.maximum(top, bot) > SKIP_LOG2)

    def trip(c):
        j, _, bot = c

        def every_row():
            _sb_blocks(whole(j), u2_ref)
            return left()

        def first_rows():
            _sb_blocks(some_rows(j, 0, TOP_ROWS), u2_ref)
            return _carry_max(carries, 0, TOP_ROWS), bot

        return (j + 1,) + lax.cond(bot > SKIP_LOG2, every_row, first_rows)

    first_trip = jnp.where(ahead, 2, 0).astype(jnp.int32)
    j, m_top, m_bot = lax.while_loop(trip_cond, trip, (first_trip, m_top, m_bot))
    for d in range(n_sub - 1):
        def tail(d=d):
            _sb_blocks(whole(blk0 + d, range(d + 1, n_sub)), u2_ref)
            return left()
        m_top, m_bot = lax.cond(
            jnp.logical_and(j >= blk0, jnp.maximum(m_top, m_bot) > SKIP_LOG2),
            tail, lambda: (m_top, m_bot))
    for i in range(n_sub):
        _sb_finish(h_cur, slice(i * TQ, (i + 1) * TQ), acc_scr.at[i], mix_scr)

    for c in range(PROJ_SPLIT, N_CHUNKS):
        _project(xb_next, w_in_ref, h_next, c)
    u = (_hcols(h_cur, slice(0, tm), COL_CA, CONV_WIDTH)
         * _sigmoid(_hcols(h_cur, slice(0, tm), COL_CB, CONV_WIDTH)))
    upad_scr[HALO:HALO + tm, :] = u

    for r in range(0, tm, CONV_ROWS):
        _conv_rows(h_cur, r, upad_scr, r, CONV_ROWS, mix_scr, cw_ref, cb_ref, cg_ref, cbeta_ref)
    last_rows = upad_scr[tm:tm + HALO, :]
    upad_scr[0:HALO, :] = last_rows
    conv_ref[0] = last_rows

    _mem_attend(h_cur, slice(0, tm), tm, mkt_scr[...], mvt_scr[...], mix_scr)
    y_ref[0] = _out_norm(x_ref[0], mix_scr, w_out_ref, lg_ref, lb_ref, alpha)


def _const_spec(shape):
    return pl.BlockSpec(shape, lambda *_: (0,) * len(shape), pipeline_mode=pl.Buffered(1))


def _prompt_call(x, mem, w_in, w_kvt, w_memt, w_out, u2, cw, cb, cg, cbeta, lg, lb, alpha, tm):
    B, T, _ = x.shape
    assert T % tm == 0 and tm % TQ == 0 and TQ == TK
    grid = (B, T // tm)
    row_spec = lambda width: pl.BlockSpec((1, tm, width), lambda b, t: (b, t, 0))
    col_spec = lambda height: pl.BlockSpec((1, height, tm), lambda b, t: (b, 0, t))
    per_b = lambda r, width: pl.BlockSpec((1, r, width), lambda b, t: (b, 0, 0))
    out_shape = (
        jax.ShapeDtypeStruct((B, T, D_MODEL), F32),
        jax.ShapeDtypeStruct((B, SB_WIDTH, T), F32),
        jax.ShapeDtypeStruct((B, SB_WIDTH, T), F32),
        jax.ShapeDtypeStruct((B, HALO, CONV_WIDTH), F32),
        jax.ShapeDtypeStruct((B, MEM_WIDTH, N_MEM), F32),
        jax.ShapeDtypeStruct((B, MEM_WIDTH, N_MEM), F32),
    )
    scratch = [
        pltpu.VMEM((2, N_CHUNKS, tm, PROJ_COLS), F32),
        pltpu.VMEM((T // TK, SB_WIDTH, TK), BF16),
        pltpu.VMEM((T // TK, SB_WIDTH, TK), BF16),
        pltpu.VMEM((MEM_WIDTH, N_MEM), BF16),
        pltpu.VMEM((MEM_WIDTH, N_MEM), BF16),
        pltpu.VMEM((tm // TQ, PAIRS, 2 * TQ, LANES), BF16),
        pltpu.VMEM((tm // TQ, PAIRS, 2 * TQ, LANES), F32),
        pltpu.VMEM((tm // TQ, PAIRS, 2 * TQ, LANES), F32),
        pltpu.VMEM((HALO + tm, CONV_WIDTH), F32),
        pltpu.VMEM((tm, D_MODEL), BF16),
    ]
    return pl.pallas_call(
        functools.partial(_prompt_kernel, tm=tm, alpha=alpha),
        grid=grid,
        in_specs=[
            row_spec(D_MODEL),
            pl.BlockSpec((1, tm, D_MODEL), lambda b, t: (b, jnp.minimum(t + 1, T // tm - 1), 0)),
            per_b(N_MEM, D_MODEL),
            _const_spec((D_MODEL, H_WIDTH)),
            _const_spec((2 * SB_WIDTH, D_MODEL)),
            _const_spec((2 * MEM_WIDTH, D_MODEL)),
            _const_spec((D_MODEL, D_MODEL)),
            _const_spec((TK, 2 * TK)),
            _const_spec((CONV_K, CONV_WIDTH)),
            _const_spec((1, CONV_WIDTH)),
            _const_spec((1, CONV_WIDTH)),
            _const_spec((1, CONV_WIDTH)),
            _const_spec((1, D_MODEL)),
            _const_spec((1, D_MODEL)),
        ],
        out_specs=(
            row_spec(D_MODEL), col_spec(SB_WIDTH), col_spec(SB_WIDTH),
            per_b(HALO, CONV_WIDTH), per_b(MEM_WIDTH, N_MEM), per_b(MEM_WIDTH, N_MEM),
        ),
        out_shape=out_shape,
        scratch_shapes=scratch,
        compiler_params=pltpu.CompilerParams(
            dimension_semantics=("arbitrary", "arbitrary"),
            vmem_limit_bytes=VMEM_LIMIT),
        name="prompt_layer",
    )(x, x, mem, w_in, w_kvt, w_memt, w_out, u2, cw, cb, cg, cbeta, lg, lb)


def _sample_kernel(x_ref, ktwin_ref, vtwin_ref, ktc_hbm, vtc_hbm, cpast_ref, mktc_ref, mvtc_ref,
                   w_in_ref, w_kvt_ref, w_out_ref, u2_ref, cw_ref, cb_ref, cg_ref, cbeta_ref,
                   lg_ref, lb_ref,
                   y_ref, k_ref, v_ref, conv_ref,
                   h_scr, ktnew_scr, vtnew_scr, ktbuf, vtbuf, sem, qm_scr, acc_scr, carry_scr,
                   upad_scr, mix_scr, *, group, n_new, win, past, alpha):
    step = pl.program_id(0)
    row, col = _stacked_index(n_new)
    n_win = win // TK
    n_blocks = past // TK

    xb = x_ref[...].astype(BF16)
    for c in range(IN_WIDTH // PROJ_COLS):
        _project(xb, w_in_ref, h_scr, c)
    for c in range(SB_WIDTH // PROJ_COLS):
        cols = slice(c * PROJ_COLS, (c + 1) * PROJ_COLS)
        k_ref[:, cols] = h_scr[COL_K // PROJ_COLS + c]
        v_ref[:, cols] = h_scr[COL_V // PROJ_COLS + c]
    kvt = lax.dot_general(w_kvt_ref[...], xb, NT_DIMS, preferred_element_type=F32)
    ktnew_scr[...] = kvt[:SB_WIDTH].astype(BF16)
    vtnew_scr[...] = kvt[SB_WIDTH:].astype(BF16)

    for g in range(group):
        _build_qm(h_scr, slice(g * n_new, (g + 1) * n_new), qm_scr.at[g])
    acc_scr[...] = jnp.zeros(acc_scr.shape, F32)
    carry_scr[...] = jnp.zeros(carry_scr.shape, F32)
    carries = [carry_scr.at[g] for g in range(group)]

    def part(g, get_kt, get_vt, mask):
        return _whole_part(qm_scr.at[g], acc_scr.at[g], carry_scr.at[g], get_kt, get_vt, mask)

    def own_past(g):
        rel = col - g * n_new
        return jnp.logical_and(rel >= 0, rel < row)

    _sb_blocks([part(g, lambda p: ktnew_scr[p * LANES:(p + 1) * LANES, :],
                     lambda p: vtnew_scr[p * LANES:(p + 1) * LANES, :], own_past(g))
                for g in range(group)], u2_ref)

    def win_blocks(j):
        cols = slice(win - (j + 1) * TK, win - j * TK)
        _sb_blocks([part(g, lambda p, g=g: ktwin_ref[g, p * LANES:(p + 1) * LANES, cols].astype(BF16),
                         lambda p, g=g: vtwin_ref[g, p * LANES:(p + 1) * LANES, cols].astype(BF16),
                         None)
                    for g in range(group)], u2_ref)
        return _carry_max(carries)

    m = jnp.float32(0.0)
    for j in range(n_win):
        m = lax.cond(m > SKIP_LOG2, functools.partial(win_blocks, j), lambda m=m: m)

    def far_walk(g, _):
        stream = step * group + g

        def far_block(j):
            start = pl.multiple_of(past - (j + 1) * TK, TK)
            ck = pltpu.make_async_copy(ktc_hbm.at[stream, :, pl.ds(start, TK)], ktbuf, sem.at[0])
            cv = pltpu.make_async_copy(vtc_hbm.at[stream, :, pl.ds(start, TK)], vtbuf, sem.at[1])
            ck.start()
            cv.start()
            ck.wait()
            cv.wait()
            _sb_blocks([part(g, lambda p: ktbuf[p * LANES:(p + 1) * LANES, :].astype(BF16),
                             lambda p: vtbuf[p * LANES:(p + 1) * LANES, :].astype(BF16), None)],
                       u2_ref)

        own = [carry_scr.at[g]]
        _sb_walk(jnp.int32(n_win), n_blocks, _carry_max(own), far_block, own)
        return 0

    lax.fori_loop(0, group, far_walk, 0)
    for g in range(group):
        rows = slice(g * n_new, (g + 1) * n_new)
        _sb_finish(h_scr, rows, acc_scr.at[g], mix_scr)
        _mem_attend(h_scr, rows, n_new, mktc_ref[g].astype(BF16), mvtc_ref[g].astype(BF16), mix_scr)

    u = (_hcols(h_scr, slice(0, group * n_new), COL_CA, CONV_WIDTH)
         * _sigmoid(_hcols(h_scr, slice(0, group * n_new), COL_CB, CONV_WIDTH)))
    for g in range(group):
        base = g * (HALO + n_new)
        upad_scr[base:base + HALO, :] = cpast_ref[g]
        upad_scr[base + HALO:base + HALO + n_new, :] = u[g * n_new:(g + 1) * n_new]
        _conv_rows(h_scr, g * n_new, upad_scr, base, n_new, mix_scr, cw_ref, cb_ref, cg_ref, cbeta_ref)
        conv_ref[g] = upad_scr[base + n_new:base + n_new + HALO, :]

    y_ref[...] = _out_norm(x_ref[...], mix_scr, w_out_ref, lg_ref, lb_ref, alpha)


def _sample_call(x, ktc, vtc, cpast, mktc, mvtc, w_in, w_kvt, w_out, u2, cw, cb, cg, cbeta, lg, lb,
                 alpha, win):
    S, n_new, _ = x.shape
    past = ktc.shape[2]
    group = TK // n_new
    assert group * n_new == TK and S % group == 0 and past % win == 0 and win % TK == 0
    assert n_new % 16 == 0 and n_new >= CONV_STATE
    rows = group * n_new
    x2 = x.reshape(S * n_new, D_MODEL)
    row_spec = lambda width: pl.BlockSpec((rows, width), lambda i: (i, 0))
    grp = lambda r, width: pl.BlockSpec((group, r, width), lambda i: (i, 0, 0))
    win_spec = pl.BlockSpec((group, SB_WIDTH, win), lambda i: (i, 0, past // win - 1))
    any_spec = pl.BlockSpec(memory_space=pl.ANY)
    out_shape = (
        jax.ShapeDtypeStruct((S * n_new, D_MODEL), F32),
        jax.ShapeDtypeStruct((S * n_new, SB_WIDTH), F32),
        jax.ShapeDtypeStruct((S * n_new, SB_WIDTH), F32),
        jax.ShapeDtypeStruct((S, HALO, CONV_WIDTH), F32),
    )
    scratch = [
        pltpu.VMEM((IN_WIDTH // PROJ_COLS, rows, PROJ_COLS), F32),
        pltpu.VMEM((SB_WIDTH, TK), BF16),
        pltpu.VMEM((SB_WIDTH, TK), BF16),
        pltpu.VMEM((SB_WIDTH, TK), F32),
        pltpu.VMEM((SB_WIDTH, TK), F32),
        pltpu.SemaphoreType.DMA((2,)),
        pltpu.VMEM((group, PAIRS, 2 * n_new, LANES), BF16),
        pltpu.VMEM((group, PAIRS, 2 * n_new, LANES), F32),
        pltpu.VMEM((group, PAIRS, 2 * n_new, LANES), F32),
        pltpu.VMEM((group * (HALO + n_new), CONV_WIDTH), F32),
        pltpu.VMEM((rows, D_MODEL), BF16),
    ]
    return pl.pallas_call(
        functools.partial(_sample_kernel, group=group, n_new=n_new, win=win, past=past, alpha=alpha),
        grid=(S // group,),
        in_specs=[
            row_spec(D_MODEL), win_spec, win_spec, any_spec, any_spec,
            grp(HALO, CONV_WIDTH), grp(MEM_WIDTH, N_MEM), grp(MEM_WIDTH, N_MEM),
            _const_spec((D_MODEL, IN_WIDTH)),
            _const_spec((2 * SB_WIDTH, D_MODEL)),
            _const_spec((D_MODEL, D_MODEL)),
            _const_spec((TK, 2 * TK)),
            _const_spec((CONV_K, CONV_WIDTH)),
            _const_spec((1, CONV_WIDTH)),
            _const_spec((1, CONV_WIDTH)),
            _const_spec((1, CONV_WIDTH)),
            _const_spec((1, D_MODEL)),
            _const_spec((1, D_MODEL)),
        ],
        out_specs=(row_spec(D_MODEL), row_spec(SB_WIDTH), row_spec(SB_WIDTH), grp(HALO, CONV_WIDTH)),
        out_shape=out_shape,
        scratch_shapes=scratch,
        compiler_params=pltpu.CompilerParams(
            dimension_semantics=("arbitrary",),
            vmem_limit_bytes=VMEM_LIMIT),
        name="sample_layer",
    )(x2, ktc, vtc, ktc, vtc, cpast, mktc, mvtc, w_in, w_kvt, w_out, u2, cw, cb, cg, cbeta, lg, lb)


def _prefix_matrix():
    j = lax.broadcasted_iota(jnp.int32, (TK, TK), 0)
    s = lax.broadcasted_iota(jnp.int32, (TK, TK), 1)
    return jnp.concatenate([(j > s).astype(BF16), jnp.ones((TK, TK), BF16)], axis=1)


def _time_minor(a):
    n, time, heads, dim = a.shape
    return jnp.transpose(a, (0, 2, 3, 1)).reshape(n, heads * dim, time)


def _time_major(a, heads):
    n, width, time = a.shape
    return jnp.transpose(a.reshape(n, heads, width // heads, time), (0, 3, 1, 2))[None]


def kernel(x_prompt, x_sample, cache_sb_k, cache_sb_v, cache_conv, cache_mem_k, cache_mem_v, mem_prompt, w_in, w_mem_kv, conv_w, conv_b, conv_ln_g, conv_ln_b, w_out, ln_g, ln_b):
    depth = w_in.shape[0]
    assert depth == 1, "single layer only"
    alpha = (2 * depth) ** 0.25
    S, n_new, _ = x_sample.shape

    u2 = _prefix_matrix()
    w = w_in[0]
    w_kv = w[:, REF_K:REF_V_END]
    w_in_b = jnp.concatenate([w[:, :REF_K], w[:, REF_V_END:], w_kv], axis=1).astype(BF16)
    w_kvt = w_kv.T.astype(BF16)
    w_memt = w_mem_kv[0].T.astype(BF16)
    w_out_b = w_out[0].astype(BF16)
    cw = conv_w[0]
    cb = conv_b[0].reshape(1, CONV_WIDTH)
    cg = conv_ln_g[0].reshape(1, CONV_WIDTH)
    cbeta = conv_ln_b[0].reshape(1, CONV_WIDTH)
    lg = ln_g[0].reshape(1, D_MODEL)
    lb = ln_b[0].reshape(1, D_MODEL)

    yp, ktp, vtp, cp, mktp, mvtp = _prompt_call(
        x_prompt, mem_prompt, w_in_b, w_kvt, w_memt, w_out_b, u2, cw, cb, cg, cbeta, lg, lb,
        alpha, tm=256)

    cpast = jnp.pad(cache_conv[0], ((0, 0), (HALO_PAD, 0), (0, 0)))
    ys, ks, vs, cs = _sample_call(
        x_sample, _time_minor(cache_sb_k[0]), _time_minor(cache_sb_v[0]), cpast,
        _time_minor(cache_mem_k[0]), _time_minor(cache_mem_v[0]),
        w_in_b, w_kvt, w_out_b, u2, cw, cb, cg, cbeta, lg, lb, alpha, win=256)

    return (
        yp,
        ys.reshape(S, n_new, D_MODEL),
        _time_major(ktp, SB_HEADS),
        _time_major(vtp, SB_HEADS),
        cp[None, :, HALO_PAD:, :],
        _time_major(mktp, MEM_HEADS),
        _time_major(mvtp, MEM_HEADS),
        ks.reshape(1, S, n_new, SB_HEADS, HEAD_DIM),
        vs.reshape(1, S, n_new, SB_HEADS, HEAD_DIM),
        cs[None, :, HALO_PAD:, :],
    )
```

```python
import functools
import math
from typing import Any, Callable, NamedTuple

import jax
import jax.numpy as jnp
from jax import lax
from jax.experimental import pallas as pl
from jax.experimental.pallas import tpu as pltpu

F32 = jnp.float32
BF16 = jnp.bfloat16

D_MODEL = 1024
SB_HEADS = 8
HEAD_DIM = 64
SB_WIDTH = SB_HEADS * HEAD_DIM
CONV_WIDTH = 256
CONV_K = 31
CONV_STATE = CONV_K - 1
MEM_HEADS = 4
MEM_WIDTH = MEM_HEADS * HEAD_DIM
N_MEM = 256
IN_WIDTH = 4 * SB_WIDTH + 3 * CONV_WIDTH + 2 * MEM_WIDTH
REF_K, REF_V_END = SB_WIDTH, 3 * SB_WIDTH
COL_Q = 0
COL_GA = SB_WIDTH
COL_CA = COL_GA + SB_WIDTH
COL_CB = COL_CA + CONV_WIDTH
COL_GC = COL_CB + CONV_WIDTH
COL_QM = COL_GC + CONV_WIDTH
COL_GM = COL_QM + MEM_WIDTH
H_WIDTH = COL_GM + MEM_WIDTH
COL_K = H_WIDTH
COL_V = COL_K + SB_WIDTH
COL_MIX_A, COL_MIX_C, COL_MIX_M = 0, SB_WIDTH, SB_WIDTH + CONV_WIDTH
LN_EPS = 1e-5
QK_SCALE = HEAD_DIM ** -0.5

LANES = 128
SUBLANES = 8
PAIRS = SB_HEADS // 2
TQ = 128
TK = 128
HALO = 32
HALO_PAD = HALO - CONV_STATE
CONV_ROWS = 64
PROJ_COLS = 256
SB_WAVE = 2
SB_SKEW = 2
TOP_ROWS = 32
N_CHUNKS = H_WIDTH // PROJ_COLS
PROJ_SPLIT = 4
PROMPT_TILE = 2 * TQ
CACHE_WINDOW = 2 * TK
INV_LN2 = 1.0 / math.log(2.0)
QK_SCALE_LOG2 = QK_SCALE * INV_LN2
SKIP_LOG2 = -152.0
VMEM_LIMIT = 60 * 1024 * 1024

NT_DIMS = (((1,), (1,)), ((), ()))


def _sigmoid(x):
    return 1.0 / (1.0 + jnp.exp(-x))


def _silu(x):
    return x * _sigmoid(x)


def _project(xb, w_in_ref, h_ref, chunk):
    cols = slice(chunk * PROJ_COLS, (chunk + 1) * PROJ_COLS)
    h_ref[chunk] = jnp.dot(xb, w_in_ref[:, cols], preferred_element_type=F32)


def _hcols(h_ref, rows, col, width):
    off = col % PROJ_COLS
    assert off + width <= PROJ_COLS
    return h_ref[col // PROJ_COLS, rows, off:off + width]


def _lane_lo(n_rows):
    return lax.broadcasted_iota(jnp.int32, (n_rows, LANES), 1) < HEAD_DIM


def _build_qm(h_ref, rows, qm_ref):
    n = qm_ref.shape[1] // 2
    lane_lo = _lane_lo(n)
    for p in range(PAIRS):
        qp = _hcols(h_ref, rows, COL_Q + p * LANES, LANES) * QK_SCALE_LOG2
        qm_ref[p, 0:n] = jnp.where(lane_lo, qp, 0.0).astype(BF16)
        qm_ref[p, n:2 * n] = jnp.where(lane_lo, 0.0, qp).astype(BF16)


class _Part(NamedTuple):
    qm: Callable[[int], Any]
    carry_get: Callable[[int], Any]
    carry_set: Callable[[int, Any], None]
    acc_add: Callable[[int, Any], None]
    get_kt: Callable[[int], Any]
    get_vt: Callable[[int], Any]
    mask: Any


def _whole_part(qm_ref, acc_ref, carry_ref, get_kt, get_vt, mask):
    def carry_set(p, v):
        carry_ref[p] = v

    def acc_add(p, v):
        acc_ref[p] += v

    return _Part(lambda p: qm_ref[p], lambda p: carry_ref[p], carry_set, acc_add,
                 get_kt, get_vt, mask)


def _rows_part(qm_ref, acc_ref, carry_ref, get_kt, get_vt, lo, hi):
    n = qm_ref.shape[1] // 2
    k = hi - lo
    head_a, head_b = slice(lo, hi), slice(n + lo, n + hi)

    def gather(ref, p):
        return jnp.concatenate([ref[p, head_a], ref[p, head_b]], axis=0)

    def carry_set(p, v):
        carry_ref[p, head_a] = v[:k]
        carry_ref[p, head_b] = v[k:]

    def acc_add(p, v):
        acc_ref[p, head_a] += v[:k]
        acc_ref[p, head_b] += v[k:]

    return _Part(lambda p: gather(qm_ref, p), lambda p: gather(carry_ref, p), carry_set, acc_add,
                 get_kt, get_vt, None)


def _sb_blocks(parts, u2_ref):
    chains = [(part, p) for part in parts for p in range(PAIRS)]
    waves = [chains[i:i + SB_WAVE] for i in range(0, len(chains), SB_WAVE)]

    def scores(wave):
        return [jnp.dot(part.qm(p), part.get_kt(p), preferred_element_type=F32)
                for part, p in wave]

    def prefix_sums(wave, zs):
        hits, fails = [], []
        for (part, _), z in zip(wave, zs):
            nz = -z
            soft = jnp.log(1.0 + jnp.exp2(jnp.minimum(z, nz))) * INV_LN2
            log_fail = jnp.minimum(nz, 0.0) - soft
            hits.append(log_fail + z)
            if part.mask is not None:
                log_fail = jnp.where(part.mask, log_fail, 0.0)
            fails.append(log_fail.astype(BF16))
        sums = jnp.dot(jnp.concatenate(fails, axis=0), u2_ref[...],
                       preferred_element_type=F32)
        return hits, sums

    def accumulate(wave, hits, sums):
        row = 0
        for (part, p), log_hit in zip(wave, hits):
            s = sums[row:row + log_hit.shape[0]]
            row += log_hit.shape[0]
            carry = part.carry_get(p)
            w = jnp.exp2(log_hit + s[:, :TK] + carry)
            if part.mask is not None:
                w = jnp.where(part.mask, w, 0.0)
            part.carry_set(p, carry + s[:, TK:])
            part.acc_add(p, lax.dot_general(w.astype(BF16), part.get_vt(p), NT_DIMS,
                                            preferred_element_type=F32))

    zs, summed = {}, {}
    for k in range(len(waves) + 2 * SB_SKEW):
        if k < len(waves):
            zs[k] = scores(waves[k])
        if 0 <= k - SB_SKEW < len(waves):
            summed[k - SB_SKEW] = prefix_sums(waves[k - SB_SKEW], zs.pop(k - SB_SKEW))
        if 0 <= k - 2 * SB_SKEW < len(waves):
            accumulate(waves[k - 2 * SB_SKEW], *summed.pop(k - 2 * SB_SKEW))


def _carry_max(carry_refs, lo=0, hi=None):
    n = carry_refs[0].shape[1] // 2
    hi = n if hi is None else hi
    m = None
    for ref in carry_refs:
        for p in range(PAIRS):
            for rows in (slice(lo, hi), slice(n + lo, n + hi)):
                m = ref[p, rows] if m is None else jnp.maximum(m, ref[p, rows])
    return jnp.max(m)


def _sb_walk(first, n_blocks, m0, block_fn, carry_refs):
    def cond(c):
        j, m = c
        return jnp.logical_and(j < n_blocks, m > SKIP_LOG2)

    def body(c):
        j, _ = c
        block_fn(j)
        return j + 1, _carry_max(carry_refs)

    return lax.while_loop(cond, body, (first, m0))


def _sb_finish(h_ref, rows, acc_ref, mix_ref):
    n = acc_ref.shape[1] // 2
    lane_lo = _lane_lo(n)
    for p in range(PAIRS):
        o = jnp.where(lane_lo, acc_ref[p, 0:n], acc_ref[p, n:2 * n])
        g = _hcols(h_ref, rows, COL_GA + p * LANES, LANES)
        mix_ref[rows, COL_MIX_A + p * LANES:COL_MIX_A + (p + 1) * LANES] = (
            o * _silu(g)).astype(BF16)


def _stacked_index(n_rows):
    row = lax.broadcasted_iota(jnp.int32, (2 * n_rows, TK), 0)
    col = lax.broadcasted_iota(jnp.int32, (2 * n_rows, TK), 1)
    return jnp.where(row >= n_rows, row - n_rows, row), col


def _conv_rows(h_ref, row, upad_ref, urow, nr, mix_ref, cw_ref, cb_ref, g_ref, b_ref):
    c = jnp.broadcast_to(cb_ref[...], (nr, CONV_WIDTH))
    for s in range(SUBLANES):
        taps = [i for i in range(CONV_K) if (HALO_PAD + i) % SUBLANES == s]
        span = nr if s == 0 else nr + SUBLANES
        part = None
        for i in taps:
            term = cw_ref[i:i + 1, :] * upad_ref[pl.ds(urow + (HALO_PAD + i - s), span), :]
            part = term if part is None else part + term
        c = c + part[s:s + nr]
    mu = jnp.mean(c, axis=-1, keepdims=True)
    d = c - mu
    var = jnp.mean(d * d, axis=-1, keepdims=True)
    n = d * lax.rsqrt(var + LN_EPS) * g_ref[...] + b_ref[...]
    gate = _hcols(h_ref, pl.ds(row, nr), COL_GC, CONV_WIDTH)
    mix_ref[pl.ds(row, nr), COL_MIX_C:COL_MIX_C + CONV_WIDTH] = (_silu(n) * _silu(gate)).astype(BF16)


def _mem_attend(h_ref, rows, n_rows, mkt, mvt, mix_ref):
    lane_lo = _lane_lo(n_rows)
    for p in range(MEM_HEADS // 2):
        qp = _hcols(h_ref, rows, COL_QM + p * LANES, LANES) * QK_SCALE
        mkp = mkt[p * LANES:(p + 1) * LANES, :]
        mvp = mvt[p * LANES:(p + 1) * LANES, :]
        outs = []
        for hh in range(2):
            qh = (jnp.where(lane_lo, qp, 0.0) if hh == 0 else jnp.where(lane_lo, 0.0, qp)).astype(BF16)
            s = jnp.dot(qh, mkp, preferred_element_type=F32)
            e = jnp.exp(s - jnp.max(s, axis=-1, keepdims=True))
            o = lax.dot_general(e.astype(BF16), mvp, NT_DIMS, preferred_element_type=F32)
            outs.append(o / jnp.sum(e, axis=-1, keepdims=True))
        o = jnp.where(lane_lo, outs[0], outs[1])
        g = _hcols(h_ref, rows, COL_GM + p * LANES, LANES)
        mix_ref[rows, COL_MIX_M + p * LANES:COL_MIX_M + (p + 1) * LANES] = (o * _silu(g)).astype(BF16)


def _out_norm(x, mix_ref, w_out_ref, g_ref, b_ref, alpha):
    r = alpha * x + jnp.dot(mix_ref[...], w_out_ref[...], preferred_element_type=F32)
    mu = jnp.mean(r, axis=-1, keepdims=True)
    d = r - mu
    var = jnp.mean(d * d, axis=-1, keepdims=True)
    return d * lax.rsqrt(var + LN_EPS) * g_ref[...] + b_ref[...]


def _prompt_kernel(x_ref, xn_ref, mem_ref, w_in_ref, w_kvt_ref, w_memt_ref, w_out_ref, u2_ref, cw_ref, cb_ref,
                   cg_ref, cbeta_ref, lg_ref, lb_ref,
                   y_ref, kt_ref, vt_ref, conv_ref, mkt_ref, mvt_ref,
                   h_scr, kt_scr, vt_scr, mkt_scr, mvt_scr, qm_scr, acc_scr, carry_scr,
                   upad_scr, mix_scr, *, tm, alpha):
    t = pl.program_id(1)
    n_sub = tm // TQ
    assert tm % CONV_ROWS == 0
    row, col = _stacked_index(TQ)
    causal = col < row

    @pl.when(t == 0)
    def _():
        kvt = lax.dot_general(w_memt_ref[...], mem_ref[0].astype(BF16), NT_DIMS,
                              preferred_element_type=F32)
        mkt_ref[0] = kvt[:MEM_WIDTH]
        mvt_ref[0] = kvt[MEM_WIDTH:]
        mkt_scr[...] = kvt[:MEM_WIDTH].astype(BF16)
        mvt_scr[...] = kvt[MEM_WIDTH:].astype(BF16)
        upad_scr[0:HALO, :] = jnp.zeros((HALO, CONV_WIDTH), F32)

    h_cur = h_scr.at[t % 2]
    h_next = h_scr.at[(t + 1) % 2]
    xb = x_ref[0].astype(BF16)

    @pl.when(t == 0)
    def _():
        for c in range(N_CHUNKS):
            _project(xb, w_in_ref, h_scr.at[0], c)

    kvt = lax.dot_general(w_kvt_ref[...], xb, NT_DIMS, preferred_element_type=F32)
    kt_ref[0] = kvt[:SB_WIDTH]
    vt_ref[0] = kvt[SB_WIDTH:]
    blk0 = t * n_sub
    for c in range(n_sub):
        kt_scr[blk0 + c] = kvt[:SB_WIDTH, c * TK:(c + 1) * TK].astype(BF16)
        vt_scr[blk0 + c] = kvt[SB_WIDTH:, c * TK:(c + 1) * TK].astype(BF16)

    for i in range(n_sub):
        _build_qm(h_cur, slice(i * TQ, (i + 1) * TQ), qm_scr.at[i])
    acc_scr[...] = jnp.zeros(acc_scr.shape, F32)
    carry_scr[...] = jnp.zeros(carry_scr.shape, F32)
    carries = [carry_scr.at[i] for i in range(n_sub)]

    def refs(i, blk):
        return (qm_scr.at[i], acc_scr.at[i], carry_scr.at[i],
                lambda p: kt_scr[blk, p * LANES:(p + 1) * LANES, :],
                lambda p: vt_scr[blk, p * LANES:(p + 1) * LANES, :])

    def whole(j, subs=range(n_sub)):
        return [_whole_part(*refs(i, blk0 + i - 1 - j), None) for i in subs]

    def some_rows(j, lo, hi):
        return [_rows_part(*refs(i, blk0 + i - 1 - j), lo, hi) for i in range(n_sub)]

    def left():
        return _carry_max(carries, 0, TOP_ROWS), _carry_max(carries, TOP_ROWS, TQ)

    xb_next = xn_ref[0].astype(BF16)
    for c in range(PROJ_SPLIT):
        _project(xb_next, w_in_ref, h_next, c)
    _sb_blocks([_whole_part(*refs(i, blk0 + i), causal) for i in range(n_sub)], u2_ref)

    def speculate():
        parts = [part for both in zip(whole(0), some_rows(1, 0, TOP_ROWS)) for part in both]
        _sb_blocks(parts, u2_ref)
        return left()

    def all_left():
        return jnp.float32(0.0), jnp.float32(0.0)

    ahead = blk0 >= 2
    m_top, m_bot = lax.cond(ahead, speculate, all_left)

    def catch_up():
        _sb_blocks(some_rows(1, TOP_ROWS, TQ), u2_ref)
        return _carry_max(carries, TOP_ROWS, TQ)

    m_bot = lax.cond(jnp.logical_and(ahead, m_bot > SKIP_LOG2), catch_up, lambda: m_bot)

    def trip_cond(c):
        j, top, bot = c
        return jnp.logical_and(j < blk0, jnp.maximum(top, bot) > SKIP_LOG2)

    def trip(c):
        j, _, bot = c

        def every_row():
            _sb_blocks(whole(j), u2_ref)
            return left()

        def first_rows():
            _sb_blocks(some_rows(j, 0, TOP_ROWS), u2_ref)
            return _carry_max(carries, 0, TOP_ROWS), bot

        return (j + 1,) + lax.cond(bot > SKIP_LOG2, every_row, first_rows)

    first_trip = jnp.where(ahead, 2, 0).astype(jnp.int32)
    j, m_top, m_bot = lax.while_loop(trip_cond, trip, (first_trip, m_top, m_bot))
    for d in range(n_sub - 1):
        def tail(d=d):
            _sb_blocks(whole(blk0 + d, range(d + 1, n_sub)), u2_ref)
            return left()
        m_top, m_bot = lax.cond(
            jnp.logical_and(j >= blk0, jnp.maximum(m_top, m_bot) > SKIP_LOG2),
            tail, lambda: (m_top, m_bot))
    for i in range(n_sub):
        _sb_finish(h_cur, slice(i * TQ, (i + 1) * TQ), acc_scr.at[i], mix_scr)

    for c in range(PROJ_SPLIT, N_CHUNKS):
        _project(xb_next, w_in_ref, h_next, c)
    u = (_hcols(h_cur, slice(0, tm), COL_CA, CONV_WIDTH)
         * _sigmoid(_hcols(h_cur, slice(0, tm), COL_CB, CONV_WIDTH)))
    upad_scr[HALO:HALO + tm, :] = u

    for r in range(0, tm, CONV_ROWS):
        _conv_rows(h_cur, r, upad_scr, r, CONV_ROWS, mix_scr, cw_ref, cb_ref, cg_ref, cbeta_ref)
    last_rows = upad_scr[tm:tm + HALO, :]
    upad_scr[0:HALO, :] = last_rows
    conv_ref[0] = last_rows

    _mem_attend(h_cur, slice(0, tm), tm, mkt_scr[...], mvt_scr[...], mix_scr)
    y_ref[0] = _out_norm(x_ref[0], mix_scr, w_out_ref, lg_ref, lb_ref, alpha)


def _const_spec(shape):
    return pl.BlockSpec(shape, lambda *_: (0,) * len(shape), pipeline_mode=pl.Buffered(1))


def _prompt_call(x, mem, w_in, w_kvt, w_memt, w_out, u2, cw, cb, cg, cbeta, lg, lb, alpha, tm):
    B, T, _ = x.shape
    assert T % tm == 0 and tm % TQ == 0 and TQ == TK
    grid = (B, T // tm)
    row_spec = lambda width: pl.BlockSpec((1, tm, width), lambda b, t: (b, t, 0))
    col_spec = lambda height: pl.BlockSpec((1, height, tm), lambda b, t: (b, 0, t))
    per_b = lambda r, width: pl.BlockSpec((1, r, width), lambda b, t: (b, 0, 0))
    out_shape = (
        jax.ShapeDtypeStruct((B, T, D_MODEL), F32),
        jax.ShapeDtypeStruct((B, SB_WIDTH, T), F32),
        jax.ShapeDtypeStruct((B, SB_WIDTH, T), F32),
        jax.ShapeDtypeStruct((B, HALO, CONV_WIDTH), F32),
        jax.ShapeDtypeStruct((B, MEM_WIDTH, N_MEM), F32),
        jax.ShapeDtypeStruct((B, MEM_WIDTH, N_MEM), F32),
    )
    scratch = [
        pltpu.VMEM((2, N_CHUNKS, tm, PROJ_COLS), F32),
        pltpu.VMEM((T // TK, SB_WIDTH, TK), BF16),
        pltpu.VMEM((T // TK, SB_WIDTH, TK), BF16),
        pltpu.VMEM((MEM_WIDTH, N_MEM), BF16),
        pltpu.VMEM((MEM_WIDTH, N_MEM), BF16),
        pltpu.VMEM((tm // TQ, PAIRS, 2 * TQ, LANES), BF16),
        pltpu.VMEM((tm // TQ, PAIRS, 2 * TQ, LANES), F32),
        pltpu.VMEM((tm // TQ, PAIRS, 2 * TQ, LANES), F32),
        pltpu.VMEM((HALO + tm, CONV_WIDTH), F32),
        pltpu.VMEM((tm, D_MODEL), BF16),
    ]
    return pl.pallas_call(
        functools.partial(_prompt_kernel, tm=tm, alpha=alpha),
        grid=grid,
        in_specs=[
            row_spec(D_MODEL),
            pl.BlockSpec((1, tm, D_MODEL), lambda b, t: (b, jnp.minimum(t + 1, T // tm - 1), 0)),
            per_b(N_MEM, D_MODEL),
            _const_spec((D_MODEL, H_WIDTH)),
            _const_spec((2 * SB_WIDTH, D_MODEL)),
            _const_spec((2 * MEM_WIDTH, D_MODEL)),
            _const_spec((D_MODEL, D_MODEL)),
            _const_spec((TK, 2 * TK)),
            _const_spec((CONV_K, CONV_WIDTH)),
            _const_spec((1, CONV_WIDTH)),
            _const_spec((1, CONV_WIDTH)),
            _const_spec((1, CONV_WIDTH)),
            _const_spec((1, D_MODEL)),
            _const_spec((1, D_MODEL)),
        ],
        out_specs=(
            row_spec(D_MODEL), col_spec(SB_WIDTH), col_spec(SB_WIDTH),
            per_b(HALO, CONV_WIDTH), per_b(MEM_WIDTH, N_MEM), per_b(MEM_WIDTH, N_MEM),
        ),
        out_shape=out_shape,
        scratch_shapes=scratch,
        compiler_params=pltpu.CompilerParams(
            dimension_semantics=("arbitrary", "arbitrary"),
            vmem_limit_bytes=VMEM_LIMIT),
        name="prompt_layer",
    )(x, x, mem, w_in, w_kvt, w_memt, w_out, u2, cw, cb, cg, cbeta, lg, lb)


def _sample_kernel(x_ref, ktwin_ref, vtwin_ref, ktc_hbm, vtc_hbm, cpast_ref, mktc_ref, mvtc_ref,
                   w_in_ref, w_kvt_ref, w_out_ref, u2_ref, cw_ref, cb_ref, cg_ref, cbeta_ref,
                   lg_ref, lb_ref,
                   y_ref, k_ref, v_ref, conv_ref,
                   h_scr, ktnew_scr, vtnew_scr, ktbuf, vtbuf, sem, qm_scr, acc_scr, carry_scr,
                   upad_scr, mix_scr, *, group, n_new, win, past, alpha):
    step = pl.program_id(0)
    row, col = _stacked_index(n_new)
    n_win = win // TK
    n_blocks = past // TK

    xb = x_ref[...].astype(BF16)
    for c in range(IN_WIDTH // PROJ_COLS):
        _project(xb, w_in_ref, h_scr, c)
    for c in range(SB_WIDTH // PROJ_COLS):
        cols = slice(c * PROJ_COLS, (c + 1) * PROJ_COLS)
        k_ref[:, cols] = h_scr[COL_K // PROJ_COLS + c]
        v_ref[:, cols] = h_scr[COL_V // PROJ_COLS + c]
    kvt = lax.dot_general(w_kvt_ref[...], xb, NT_DIMS, preferred_element_type=F32)
    ktnew_scr[...] = kvt[:SB_WIDTH].astype(BF16)
    vtnew_scr[...] = kvt[SB_WIDTH:].astype(BF16)

    for g in range(group):
        _build_qm(h_scr, slice(g * n_new, (g + 1) * n_new), qm_scr.at[g])
    acc_scr[...] = jnp.zeros(acc_scr.shape, F32)
    carry_scr[...] = jnp.zeros(carry_scr.shape, F32)
    carries = [carry_scr.at[g] for g in range(group)]

    def part(g, get_kt, get_vt, mask):
        return _whole_part(qm_scr.at[g], acc_scr.at[g], carry_scr.at[g], get_kt, get_vt, mask)

    def own_past(g):
        rel = col - g * n_new
        return jnp.logical_and(rel >= 0, rel < row)

    _sb_blocks([part(g, lambda p: ktnew_scr[p * LANES:(p + 1) * LANES, :],
                     lambda p: vtnew_scr[p * LANES:(p + 1) * LANES, :], own_past(g))
                for g in range(group)], u2_ref)

    def win_blocks(j):
        cols = slice(win - (j + 1) * TK, win - j * TK)
        _sb_blocks([part(g, lambda p, g=g: ktwin_ref[g, p * LANES:(p + 1) * LANES, cols].astype(BF16),
                         lambda p, g=g: vtwin_ref[g, p * LANES:(p + 1) * LANES, cols].astype(BF16),
                         None)
                    for g in range(group)], u2_ref)
        return _carry_max(carries)

    m = jnp.float32(0.0)
    for j in range(n_win):
        m = lax.cond(m > SKIP_LOG2, functools.partial(win_blocks, j), lambda m=m: m)

    def far_walk(g, _):
        stream = step * group + g

        def far_block(j):
            start = pl.multiple_of(past - (j + 1) * TK, TK)
            ck = pltpu.make_async_copy(ktc_hbm.at[stream, :, pl.ds(start, TK)], ktbuf, sem.at[0])
            cv = pltpu.make_async_copy(vtc_hbm.at[stream, :, pl.ds(start, TK)], vtbuf, sem.at[1])
            ck.start()
            cv.start()
            ck.wait()
            cv.wait()
            _sb_blocks([part(g, lambda p: ktbuf[p * LANES:(p + 1) * LANES, :].astype(BF16),
                             lambda p: vtbuf[p * LANES:(p + 1) * LANES, :].astype(BF16), None)],
                       u2_ref)

        own = [carry_scr.at[g]]
        _sb_walk(jnp.int32(n_win), n_blocks, _carry_max(own), far_block, own)
        return 0

    lax.fori_loop(0, group, far_walk, 0)
    for g in range(group):
        rows = slice(g * n_new, (g + 1) * n_new)
        _sb_finish(h_scr, rows, acc_scr.at[g], mix_scr)
        _mem_attend(h_scr, rows, n_new, mktc_ref[g].astype(BF16), mvtc_ref[g].astype(BF16), mix_scr)

    u = (_hcols(h_scr, slice(0, group * n_new), COL_CA, CONV_WIDTH)
         * _sigmoid(_hcols(h_scr, slice(0, group * n_new), COL_CB, CONV_WIDTH)))
    for g in range(group):
        base = g * (HALO + n_new)
        upad_scr[base:base + HALO, :] = cpast_ref[g]
        upad_scr[base + HALO:base + HALO + n_new, :] = u[g * n_new:(g + 1) * n_new]
        _conv_rows(h_scr, g * n_new, upad_scr, base, n_new, mix_scr, cw_ref, cb_ref, cg_ref, cbeta_ref)
        conv_ref[g] = upad_scr[base + n_new:base + n_new + HALO, :]

    y_ref[...] = _out_norm(x_ref[...], mix_scr, w_out_ref, lg_ref, lb_ref, alpha)


def _sample_call(x, ktc, vtc, cpast, mktc, mvtc, w_in, w_kvt, w_out, u2, cw, cb, cg, cbeta, lg, lb,
                 alpha, win):
    S, n_new, _ = x.shape
    past = ktc.shape[2]
    group = TK // n_new
    assert group * n_new == TK and S % group == 0 and past % win == 0 and win % TK == 0
    assert n_new % 16 == 0 and n_new >= CONV_STATE
    rows = group * n_new
    x2 = x.reshape(S * n_new, D_MODEL)
    row_spec = lambda width: pl.BlockSpec((rows, width), lambda i: (i, 0))
    grp = lambda r, width: pl.BlockSpec((group, r, width), lambda i: (i, 0, 0))
    win_spec = pl.BlockSpec((group, SB_WIDTH, win), lambda i: (i, 0, past // win - 1))
    any_spec = pl.BlockSpec(memory_space=pl.ANY)
    out_shape = (
        jax.ShapeDtypeStruct((S * n_new, D_MODEL), F32),
        jax.ShapeDtypeStruct((S * n_new, SB_WIDTH), F32),
        jax.ShapeDtypeStruct((S * n_new, SB_WIDTH), F32),
        jax.ShapeDtypeStruct((S, HALO, CONV_WIDTH), F32),
    )
    scratch = [
        pltpu.VMEM((IN_WIDTH // PROJ_COLS, rows, PROJ_COLS), F32),
        pltpu.VMEM((SB_WIDTH, TK), BF16),
        pltpu.VMEM((SB_WIDTH, TK), BF16),
        pltpu.VMEM((SB_WIDTH, TK), F32),
        pltpu.VMEM((SB_WIDTH, TK), F32),
        pltpu.SemaphoreType.DMA((2,)),
        pltpu.VMEM((group, PAIRS, 2 * n_new, LANES), BF16),
        pltpu.VMEM((group, PAIRS, 2 * n_new, LANES), F32),
        pltpu.VMEM((group, PAIRS, 2 * n_new, LANES), F32),
        pltpu.VMEM((group * (HALO + n_new), CONV_WIDTH), F32),
        pltpu.VMEM((rows, D_MODEL), BF16),
    ]
    return pl.pallas_call(
        functools.partial(_sample_kernel, group=group, n_new=n_new, win=win, past=past, alpha=alpha),
        grid=(S // group,),
        in_specs=[
            row_spec(D_MODEL), win_spec, win_spec, any_spec, any_spec,
            grp(HALO, CONV_WIDTH), grp(MEM_WIDTH, N_MEM), grp(MEM_WIDTH, N_MEM),
            _const_spec((D_MODEL, IN_WIDTH)),
            _const_spec((2 * SB_WIDTH, D_MODEL)),
            _const_spec((D_MODEL, D_MODEL)),
            _const_spec((TK, 2 * TK)),
            _const_spec((CONV_K, CONV_WIDTH)),
            _const_spec((1, CONV_WIDTH)),
            _const_spec((1, CONV_WIDTH)),
            _const_spec((1, CONV_WIDTH)),
            _const_spec((1, D_MODEL)),
            _const_spec((1, D_MODEL)),
        ],
        out_specs=(row_spec(D_MODEL), row_spec(SB_WIDTH), row_spec(SB_WIDTH), grp(HALO, CONV_WIDTH)),
        out_shape=out_shape,
        scratch_shapes=scratch,
        compiler_params=pltpu.CompilerParams(
            dimension_semantics=("arbitrary",),
            vmem_limit_bytes=VMEM_LIMIT),
        name="sample_layer",
    )(x2, ktc, vtc, ktc, vtc, cpast, mktc, mvtc, w_in, w_kvt, w_out, u2, cw, cb, cg, cbeta, lg, lb)


def _prefix_matrix():
    j = lax.broadcasted_iota(jnp.int32, (TK, TK), 0)
    s = lax.broadcasted_iota(jnp.int32, (TK, TK), 1)
    return jnp.concatenate([(j > s).astype(BF16), jnp.ones((TK, TK), BF16)], axis=1)


def _time_minor(a):
    n, time, heads, dim = a.shape
    return jnp.transpose(a, (0, 2, 3, 1)).reshape(n, heads * dim, time)


def _time_major(a, heads):
    n, width, time = a.shape
    return jnp.transpose(a.reshape(n, heads, width // heads, time), (0, 3, 1, 2))[None]


def kernel(x_prompt, x_sample, cache_sb_k, cache_sb_v, cache_conv, cache_mem_k, cache_mem_v, mem_prompt, w_in, w_mem_kv, conv_w, conv_b, conv_ln_g, conv_ln_b, w_out, ln_g, ln_b):
    depth = w_in.shape[0]
    assert depth == 1, "single layer only"
    alpha = (2 * depth) ** 0.25
    S, n_new, _ = x_sample.shape

    u2 = _prefix_matrix()
    w = w_in[0]
    w_kv = w[:, REF_K:REF_V_END]
    w_in_b = jnp.concatenate([w[:, :REF_K], w[:, REF_V_END:], w_kv], axis=1).astype(BF16)
    w_kvt = w_kv.T.astype(BF16)
    w_memt = w_mem_kv[0].T.astype(BF16)
    w_out_b = w_out[0].astype(BF16)
    cw = conv_w[0]
    cb = conv_b[0].reshape(1, CONV_WIDTH)
    cg = conv_ln_g[0].reshape(1, CONV_WIDTH)
    cbeta = conv_ln_b[0].reshape(1, CONV_WIDTH)
    lg = ln_g[0].reshape(1, D_MODEL)
    lb = ln_b[0].reshape(1, D_MODEL)

    yp, ktp, vtp, cp, mktp, mvtp = _prompt_call(
        x_prompt, mem_prompt, w_in_b, w_kvt, w_memt, w_out_b, u2, cw, cb, cg, cbeta, lg, lb,
        alpha, tm=PROMPT_TILE)

    cpast = jnp.pad(cache_conv[0], ((0, 0), (HALO_PAD, 0), (0, 0)))
    ys, ks, vs, cs = _sample_call(
        x_sample, _time_minor(cache_sb_k[0]), _time_minor(cache_sb_v[0]), cpast,
        _time_minor(cache_mem_k[0]), _time_minor(cache_mem_v[0]),
        w_in_b, w_kvt, w_out_b, u2, cw, cb, cg, cbeta, lg, lb, alpha, win=CACHE_WINDOW)

    return (
        yp,
        ys.reshape(S, n_new, D_MODEL),
        _time_major(ktp, SB_HEADS),
        _time_major(vtp, SB_HEADS),
        cp[None, :, HALO_PAD:, :],
        _time_major(mktp, MEM_HEADS),
        _time_major(mvtp, MEM_HEADS),
        ks.reshape(1, S, n_new, SB_HEADS, HEAD_DIM),
        vs.reshape(1, S, n_new, SB_HEADS, HEAD_DIM),
        cs[None, :, HALO_PAD:, :],
    )
```

```python
import functools
import math
from typing import Any, Callable, NamedTuple

import jax
import jax.numpy as jnp
from jax import lax
from jax.experimental import pallas as pl
from jax.experimental.pallas import tpu as pltpu

F32 = jnp.float32
BF16 = jnp.bfloat16

D_MODEL = 1024
SB_HEADS = 8
HEAD_DIM = 64
SB_WIDTH = SB_HEADS * HEAD_DIM
CONV_WIDTH = 256
CONV_K = 31
CONV_STATE = CONV_K - 1
MEM_HEADS = 4
MEM_WIDTH = MEM_HEADS * HEAD_DIM
N_MEM = 256
IN_WIDTH = 4 * SB_WIDTH + 3 * CONV_WIDTH + 2 * MEM_WIDTH
REF_K, REF_V_END = SB_WIDTH, 3 * SB_WIDTH
COL_Q = 0
COL_GA = SB_WIDTH
COL_CA = COL_GA + SB_WIDTH
COL_CB = COL_CA + CONV_WIDTH
COL_GC = COL_CB + CONV_WIDTH
COL_QM = COL_GC + CONV_WIDTH
COL_GM = COL_QM + MEM_WIDTH
H_WIDTH = COL_GM + MEM_WIDTH
COL_K = H_WIDTH
COL_V = COL_K + SB_WIDTH
COL_MIX_A, COL_MIX_C, COL_MIX_M = 0, SB_WIDTH, SB_WIDTH + CONV_WIDTH
LN_EPS = 1e-5
QK_SCALE = HEAD_DIM ** -0.5

LANES = 128
SUBLANES = 8
PAIRS = SB_HEADS // 2
TQ = 128
TK = 128
HALO = 32
HALO_PAD = HALO - CONV_STATE
CONV_ROWS = 256
PROJ_COLS = 256
SB_WAVE = 2
SB_SKEW = 2
TOP_ROWS = 32
N_CHUNKS = H_WIDTH // PROJ_COLS
PROJ_SPLIT = 4
PROMPT_TILE = 2 * TQ
CACHE_WINDOW = 2 * TK
INV_LN2 = 1.0 / math.log(2.0)
QK_SCALE_LOG2 = QK_SCALE * INV_LN2
SKIP_LOG2 = -152.0
VMEM_LIMIT = 60 * 1024 * 1024

NT_DIMS = (((1,), (1,)), ((), ()))


def _sigmoid(x):
    return 1.0 / (1.0 + jnp.exp(-x))


def _silu(x):
    return x * _sigmoid(x)


def _project(xb, w_in_ref, h_ref, chunk):
    cols = slice(chunk * PROJ_COLS, (chunk + 1) * PROJ_COLS)
    h_ref[chunk] = jnp.dot(xb, w_in_ref[:, cols], preferred_element_type=F32)


def _hcols(h_ref, rows, col, width):
    off = col % PROJ_COLS
    assert off + width <= PROJ_COLS
    return h_ref[col // PROJ_COLS, rows, off:off + width]


def _lane_lo(n_rows):
    return lax.broadcasted_iota(jnp.int32, (n_rows, LANES), 1) < HEAD_DIM


def _build_qm(h_ref, rows, qm_ref):
    n = qm_ref.shape[1] // 2
    lane_lo = _lane_lo(n)
    for p in range(PAIRS):
        qp = _hcols(h_ref, rows, COL_Q + p * LANES, LANES) * QK_SCALE_LOG2
        qm_ref[p, 0:n] = jnp.where(lane_lo, qp, 0.0).astype(BF16)
        qm_ref[p, n:2 * n] = jnp.where(lane_lo, 0.0, qp).astype(BF16)


class _Part(NamedTuple):
    qm: Callable[[int], Any]
    carry_get: Callable[[int], Any]
    carry_set: Callable[[int, Any], None]
    acc_add: Callable[[int, Any], None]
    get_kt: Callable[[int], Any]
    get_vt: Callable[[int], Any]
    mask: Any


def _whole_part(qm_ref, acc_ref, carry_ref, get_kt, get_vt, mask):
    def carry_set(p, v):
        carry_ref[p] = v

    def acc_add(p, v):
        acc_ref[p] += v

    return _Part(lambda p: qm_ref[p], lambda p: carry_ref[p], carry_set, acc_add,
                 get_kt, get_vt, mask)


def _rows_part(qm_ref, acc_ref, carry_ref, get_kt, get_vt, lo, hi):
    n = qm_ref.shape[1] // 2
    k = hi - lo
    head_a, head_b = slice(lo, hi), slice(n + lo, n + hi)

    def gather(ref, p):
        return jnp.concatenate([ref[p, head_a], ref[p, head_b]], axis=0)

    def carry_set(p, v):
        carry_ref[p, head_a] = v[:k]
        carry_ref[p, head_b] = v[k:]

    def acc_add(p, v):
        acc_ref[p, head_a] += v[:k]
        acc_ref[p, head_b] += v[k:]

    return _Part(lambda p: gather(qm_ref, p), lambda p: gather(carry_ref, p), carry_set, acc_add,
                 get_kt, get_vt, None)


def _sb_blocks(parts, u2_ref):
    chains = [(part, p) for part in parts for p in range(PAIRS)]
    waves = [chains[i:i + SB_WAVE] for i in range(0, len(chains), SB_WAVE)]

    def scores(wave):
        return [jnp.dot(part.qm(p), part.get_kt(p), preferred_element_type=F32)
                for part, p in wave]

    def prefix_sums(wave, zs):
        hits, fails = [], []
        for (part, _), z in zip(wave, zs):
            nz = -z
            soft = jnp.log(1.0 + jnp.exp2(jnp.minimum(z, nz))) * INV_LN2
            log_fail = jnp.minimum(nz, 0.0) - soft
            hits.append(log_fail + z)
            if part.mask is not None:
                log_fail = jnp.where(part.mask, log_fail, 0.0)
            fails.append(log_fail.astype(BF16))
        sums = jnp.dot(jnp.concatenate(fails, axis=0), u2_ref[...],
                       preferred_element_type=F32)
        return hits, sums

    def accumulate(wave, hits, sums):
        row = 0
        for (part, p), log_hit in zip(wave, hits):
            s = sums[row:row + log_hit.shape[0]]
            row += log_hit.shape[0]
            carry = part.carry_get(p)
            w = jnp.exp2(log_hit + s[:, :TK] + carry)
            if part.mask is not None:
                w = jnp.where(part.mask, w, 0.0)
            part.carry_set(p, carry + s[:, TK:])
            part.acc_add(p, lax.dot_general(w.astype(BF16), part.get_vt(p), NT_DIMS,
                                            preferred_element_type=F32))

    zs, summed = {}, {}
    for k in range(len(waves) + 2 * SB_SKEW):
        if k < len(waves):
            zs[k] = scores(waves[k])
        if 0 <= k - SB_SKEW < len(waves):
            summed[k - SB_SKEW] = prefix_sums(waves[k - SB_SKEW], zs.pop(k - SB_SKEW))
        if 0 <= k - 2 * SB_SKEW < len(waves):
            accumulate(waves[k - 2 * SB_SKEW], *summed.pop(k - 2 * SB_SKEW))


def _carry_max(carry_refs, lo=0, hi=None):
    n = carry_refs[0].shape[1] // 2
    hi = n if hi is None else hi
    m = None
    for ref in carry_refs:
        for p in range(PAIRS):
            for rows in (slice(lo, hi), slice(n + lo, n + hi)):
                m = ref[p, rows] if m is None else jnp.maximum(m, ref[p, rows])
    return jnp.max(m)


def _sb_walk(first, n_blocks, m0, block_fn, carry_refs):
    def cond(c):
        j, m = c
        return jnp.logical_and(j < n_blocks, m > SKIP_LOG2)

    def body(c):
        j, _ = c
        block_fn(j)
        return j + 1, _carry_max(carry_refs)

    return lax.while_loop(cond, body, (first, m0))


def _sb_finish(h_ref, rows, acc_ref, mix_ref):
    n = acc_ref.shape[1] // 2
    lane_lo = _lane_lo(n)
    for p in range(PAIRS):
        o = jnp.where(lane_lo, acc_ref[p, 0:n], acc_ref[p, n:2 * n])
        g = _hcols(h_ref, rows, COL_GA + p * LANES, LANES)
        mix_ref[rows, COL_MIX_A + p * LANES:COL_MIX_A + (p + 1) * LANES] = (
            o * _silu(g)).astype(BF16)


def _stacked_index(n_rows):
    row = lax.broadcasted_iota(jnp.int32, (2 * n_rows, TK), 0)
    col = lax.broadcasted_iota(jnp.int32, (2 * n_rows, TK), 1)
    return jnp.where(row >= n_rows, row - n_rows, row), col


def _conv_rows(h_ref, row, upad_ref, urow, nr, mix_ref, cw_ref, cb_ref, g_ref, b_ref):
    c = jnp.broadcast_to(cb_ref[...], (nr, CONV_WIDTH))
    for s in range(SUBLANES):
        taps = [i for i in range(CONV_K) if (HALO_PAD + i) % SUBLANES == s]
        span = nr if s == 0 else nr + SUBLANES
        part = None
        for i in taps:
            term = cw_ref[i:i + 1, :] * upad_ref[pl.ds(urow + (HALO_PAD + i - s), span), :]
            part = term if part is None else part + term
        c = c + part[s:s + nr]
    mu = jnp.mean(c, axis=-1, keepdims=True)
    d = c - mu
    var = jnp.mean(d * d, axis=-1, keepdims=True)
    n = d * lax.rsqrt(var + LN_EPS) * g_ref[...] + b_ref[...]
    gate = _hcols(h_ref, pl.ds(row, nr), COL_GC, CONV_WIDTH)
    mix_ref[pl.ds(row, nr), COL_MIX_C:COL_MIX_C + CONV_WIDTH] = (_silu(n) * _silu(gate)).astype(BF16)


def _mem_attend(h_ref, rows, n_rows, mkt, mvt, mix_ref):
    lane_lo = _lane_lo(n_rows)
    for p in range(MEM_HEADS // 2):
        qp = _hcols(h_ref, rows, COL_QM + p * LANES, LANES) * QK_SCALE
        mkp = mkt[p * LANES:(p + 1) * LANES, :]
        mvp = mvt[p * LANES:(p + 1) * LANES, :]
        outs = []
        for hh in range(2):
            qh = (jnp.where(lane_lo, qp, 0.0) if hh == 0 else jnp.where(lane_lo, 0.0, qp)).astype(BF16)
            s = jnp.dot(qh, mkp, preferred_element_type=F32)
            e = jnp.exp(s - jnp.max(s, axis=-1, keepdims=True))
            o = lax.dot_general(e.astype(BF16), mvp, NT_DIMS, preferred_element_type=F32)
            outs.append(o / jnp.sum(e, axis=-1, keepdims=True))
        o = jnp.where(lane_lo, outs[0], outs[1])
        g = _hcols(h_ref, rows, COL_GM + p * LANES, LANES)
        mix_ref[rows, COL_MIX_M + p * LANES:COL_MIX_M + (p + 1) * LANES] = (o * _silu(g)).astype(BF16)


def _out_norm(x, mix_ref, w_out_ref, g_ref, b_ref, alpha):
    r = alpha * x + jnp.dot(mix_ref[...], w_out_ref[...], preferred_element_type=F32)
    mu = jnp.mean(r, axis=-1, keepdims=True)
    d = r - mu
    var = jnp.mean(d * d, axis=-1, keepdims=True)
    return d * lax.rsqrt(var + LN_EPS) * g_ref[...] + b_ref[...]


def _prompt_kernel(x_ref, xn_ref, mem_ref, w_in_ref, w_kvt_ref, w_memt_ref, w_out_ref, u2_ref, cw_ref, cb_ref,
                   cg_ref, cbeta_ref, lg_ref, lb_ref,
                   y_ref, kt_ref, vt_ref, conv_ref, mkt_ref, mvt_ref,
                   h_scr, kt_scr, vt_scr, mkt_scr, mvt_scr, qm_scr, acc_scr, carry_scr,
                   upad_scr, mix_scr, *, tm, alpha):
    t = pl.program_id(1)
    n_sub = tm // TQ
    assert tm % CONV_ROWS == 0
    row, col = _stacked_index(TQ)
    causal = col < row

    @pl.when(t == 0)
    def _():
        kvt = lax.dot_general(w_memt_ref[...], mem_ref[0].astype(BF16), NT_DIMS,
                              preferred_element_type=F32)
        mkt_ref[0] = kvt[:MEM_WIDTH]
        mvt_ref[0] = kvt[MEM_WIDTH:]
        mkt_scr[...] = kvt[:MEM_WIDTH].astype(BF16)
        mvt_scr[...] = kvt[MEM_WIDTH:].astype(BF16)
        upad_scr[0:HALO, :] = jnp.zeros((HALO, CONV_WIDTH), F32)

    h_cur = h_scr.at[t % 2]
    h_next = h_scr.at[(t + 1) % 2]
    xb = x_ref[0].astype(BF16)

    @pl.when(t == 0)
    def _():
        for c in range(N_CHUNKS):
            _project(xb, w_in_ref, h_scr.at[0], c)

    kvt = lax.dot_general(w_kvt_ref[...], xb, NT_DIMS, preferred_element_type=F32)
    kt_ref[0] = kvt[:SB_WIDTH]
    vt_ref[0] = kvt[SB_WIDTH:]
    blk0 = t * n_sub
    for c in range(n_sub):
        kt_scr[blk0 + c] = kvt[:SB_WIDTH, c * TK:(c + 1) * TK].astype(BF16)
        vt_scr[blk0 + c] = kvt[SB_WIDTH:, c * TK:(c + 1) * TK].astype(BF16)

    for i in range(n_sub):
        _build_qm(h_cur, slice(i * TQ, (i + 1) * TQ), qm_scr.at[i])
    acc_scr[...] = jnp.zeros(acc_scr.shape, F32)
    carry_scr[...] = jnp.zeros(carry_scr.shape, F32)
    carries = [carry_scr.at[i] for i in range(n_sub)]

    def refs(i, blk):
        return (qm_scr.at[i], acc_scr.at[i], carry_scr.at[i],
                lambda p: kt_scr[blk, p * LANES:(p + 1) * LANES, :],
                lambda p: vt_scr[blk, p * LANES:(p + 1) * LANES, :])

    def whole(j, subs=range(n_sub)):
        return [_whole_part(*refs(i, blk0 + i - 1 - j), None) for i in subs]

    def some_rows(j, lo, hi):
        return [_rows_part(*refs(i, blk0 + i - 1 - j), lo, hi) for i in range(n_sub)]

    def left():
        return _carry_max(carries, 0, TOP_ROWS), _carry_max(carries, TOP_ROWS, TQ)

    xb_next = xn_ref[0].astype(BF16)
    for c in range(PROJ_SPLIT):
        _project(xb_next, w_in_ref, h_next, c)
    _sb_blocks([_whole_part(*refs(i, blk0 + i), causal) for i in range(n_sub)], u2_ref)

    def speculate():
        parts = [part for both in zip(whole(0), some_rows(1, 0, TOP_ROWS)) for part in both]
        _sb_blocks(parts, u2_ref)
        return left()

    def all_left():
        return jnp.float32(0.0), jnp.float32(0.0)

    ahead = blk0 >= 2
    m_top, m_bot = lax.cond(ahead, speculate, all_left)

    def catch_up():
        _sb_blocks(some_rows(1, TOP_ROWS, TQ), u2_ref)
        return _carry_max(carries, TOP_ROWS, TQ)

    m_bot = lax.cond(jnp.logical_and(ahead, m_bot > SKIP_LOG2), catch_up, lambda: m_bot)

    def trip_cond(c):
        j, top, bot = c
        return jnp.logical_and(j < blk0, jnp.maximum(top, bot) > SKIP_LOG2)

    def trip(c):
        j, _, bot = c

        def every_row():
            _sb_blocks(whole(j), u2_ref)
            return left()

        def first_rows():
            _sb_blocks(some_rows(j, 0, TOP_ROWS), u2_ref)
            return _carry_max(carries, 0, TOP_ROWS), bot

        return (j + 1,) + lax.cond(bot > SKIP_LOG2, every_row, first_rows)

    first_trip = jnp.where(ahead, 2, 0).astype(jnp.int32)
    j, m_top, m_bot = lax.while_loop(trip_cond, trip, (first_trip, m_top, m_bot))
    for d in range(n_sub - 1):
        def tail(d=d):
            _sb_blocks(whole(blk0 + d, range(d + 1, n_sub)), u2_ref)
            return left()
        m_top, m_bot = lax.cond(
            jnp.logical_and(j >= blk0, jnp.maximum(m_top, m_bot) > SKIP_LOG2),
            tail, lambda: (m_top, m_bot))
    for c in range(PROJ_SPLIT, N_CHUNKS):
        _project(xb_next, w_in_ref, h_next, c)
    for i in range(n_sub):
        _sb_finish(h_cur, slice(i * TQ, (i + 1) * TQ), acc_scr.at[i], mix_scr)

    u = (_hcols(h_cur, slice(0, tm), COL_CA, CONV_WIDTH)
         * _sigmoid(_hcols(h_cur, slice(0, tm), COL_CB, CONV_WIDTH)))
    upad_scr[HALO:HALO + tm, :] = u

    for r in range(0, tm, CONV_ROWS):
        _conv_rows(h_cur, r, upad_scr, r, CONV_ROWS, mix_scr, cw_ref, cb_ref, cg_ref, cbeta_ref)
    last_rows = upad_scr[tm:tm + HALO, :]
    upad_scr[0:HALO, :] = last_rows
    conv_ref[0] = last_rows

    _mem_attend(h_cur, slice(0, tm), tm, mkt_scr[...], mvt_scr[...], mix_scr)
    y_ref[0] = _out_norm(x_ref[0], mix_scr, w_out_ref, lg_ref, lb_ref, alpha)


def _const_spec(shape):
    return pl.BlockSpec(shape, lambda *_: (0,) * len(shape), pipeline_mode=pl.Buffered(1))


def _prompt_call(x, mem, w_in, w_kvt, w_memt, w_out, u2, cw, cb, cg, cbeta, lg, lb, alpha, tm):
    B, T, _ = x.shape
    assert T % tm == 0 and tm % TQ == 0 and TQ == TK
    grid = (B, T // tm)
    row_spec = lambda width: pl.BlockSpec((1, tm, width), lambda b, t: (b, t, 0))
    col_spec = lambda height: pl.BlockSpec((1, height, tm), lambda b, t: (b, 0, t))
    per_b = lambda r, width: pl.BlockSpec((1, r, width), lambda b, t: (b, 0, 0))
    out_shape = (
        jax.ShapeDtypeStruct((B, T, D_MODEL), F32),
        jax.ShapeDtypeStruct((B, SB_WIDTH, T), F32),
        jax.ShapeDtypeStruct((B, SB_WIDTH, T), F32),
        jax.ShapeDtypeStruct((B, HALO, CONV_WIDTH), F32),
        jax.ShapeDtypeStruct((B, MEM_WIDTH, N_MEM), F32),
        jax.ShapeDtypeStruct((B, MEM_WIDTH, N_MEM), F32),
    )
    scratch = [
        pltpu.VMEM((2, N_CHUNKS, tm, PROJ_COLS), F32),
        pltpu.VMEM((T // TK, SB_WIDTH, TK), BF16),
        pltpu.VMEM((T // TK, SB_WIDTH, TK), BF16),
        pltpu.VMEM((MEM_WIDTH, N_MEM), BF16),
        pltpu.VMEM((MEM_WIDTH, N_MEM), BF16),
        pltpu.VMEM((tm // TQ, PAIRS, 2 * TQ, LANES), BF16),
        pltpu.VMEM((tm // TQ, PAIRS, 2 * TQ, LANES), F32),
        pltpu.VMEM((tm // TQ, PAIRS, 2 * TQ, LANES), F32),
        pltpu.VMEM((HALO + tm, CONV_WIDTH), F32),
        pltpu.VMEM((tm, D_MODEL), BF16),
    ]
    return pl.pallas_call(
        functools.partial(_prompt_kernel, tm=tm, alpha=alpha),
        grid=grid,
        in_specs=[
            row_spec(D_MODEL),
            pl.BlockSpec((1, tm, D_MODEL), lambda b, t: (b, jnp.minimum(t + 1, T // tm - 1), 0)),
            per_b(N_MEM, D_MODEL),
            _const_spec((D_MODEL, H_WIDTH)),
            _const_spec((2 * SB_WIDTH, D_MODEL)),
            _const_spec((2 * MEM_WIDTH, D_MODEL)),
            _const_spec((D_MODEL, D_MODEL)),
            _const_spec((TK, 2 * TK)),
            _const_spec((CONV_K, CONV_WIDTH)),
            _const_spec((1, CONV_WIDTH)),
            _const_spec((1, CONV_WIDTH)),
            _const_spec((1, CONV_WIDTH)),
            _const_spec((1, D_MODEL)),
            _const_spec((1, D_MODEL)),
        ],
        out_specs=(
            row_spec(D_MODEL), col_spec(SB_WIDTH), col_spec(SB_WIDTH),
            per_b(HALO, CONV_WIDTH), per_b(MEM_WIDTH, N_MEM), per_b(MEM_WIDTH, N_MEM),
        ),
        out_shape=out_shape,
        scratch_shapes=scratch,
        compiler_params=pltpu.CompilerParams(
            dimension_semantics=("arbitrary", "arbitrary"),
            vmem_limit_bytes=VMEM_LIMIT),
        name="prompt_layer",
    )(x, x, mem, w_in, w_kvt, w_memt, w_out, u2, cw, cb, cg, cbeta, lg, lb)


def _sample_kernel(x_ref, ktwin_ref, vtwin_ref, ktc_hbm, vtc_hbm, cpast_ref, mktc_ref, mvtc_ref,
                   w_in_ref, w_kvt_ref, w_out_ref, u2_ref, cw_ref, cb_ref, cg_ref, cbeta_ref,
                   lg_ref, lb_ref,
                   y_ref, k_ref, v_ref, conv_ref,
                   h_scr, ktnew_scr, vtnew_scr, ktbuf, vtbuf, sem, qm_scr, acc_scr, carry_scr,
                   upad_scr, mix_scr, *, group, n_new, win, past, alpha):
    step = pl.program_id(0)
    row, col = _stacked_index(n_new)
    n_win = win // TK
    n_blocks = past // TK

    xb = x_ref[...].astype(BF16)
    for c in range(IN_WIDTH // PROJ_COLS):
        _project(xb, w_in_ref, h_scr, c)
    for c in range(SB_WIDTH // PROJ_COLS):
        cols = slice(c * PROJ_COLS, (c + 1) * PROJ_COLS)
        k_ref[:, cols] = h_scr[COL_K // PROJ_COLS + c]
        v_ref[:, cols] = h_scr[COL_V // PROJ_COLS + c]
    kvt = lax.dot_general(w_kvt_ref[...], xb, NT_DIMS, preferred_element_type=F32)
    ktnew_scr[...] = kvt[:SB_WIDTH].astype(BF16)
    vtnew_scr[...] = kvt[SB_WIDTH:].astype(BF16)

    for g in range(group):
        _build_qm(h_scr, slice(g * n_new, (g + 1) * n_new), qm_scr.at[g])
    acc_scr[...] = jnp.zeros(acc_scr.shape, F32)
    carry_scr[...] = jnp.zeros(carry_scr.shape, F32)
    carries = [carry_scr.at[g] for g in range(group)]

    def part(g, get_kt, get_vt, mask):
        return _whole_part(qm_scr.at[g], acc_scr.at[g], carry_scr.at[g], get_kt, get_vt, mask)

    def own_past(g):
        rel = col - g * n_new
        return jnp.logical_and(rel >= 0, rel < row)

    _sb_blocks([part(g, lambda p: ktnew_scr[p * LANES:(p + 1) * LANES, :],
                     lambda p: vtnew_scr[p * LANES:(p + 1) * LANES, :], own_past(g))
                for g in range(group)], u2_ref)

    def win_blocks(j):
        cols = slice(win - (j + 1) * TK, win - j * TK)
        _sb_blocks([part(g, lambda p, g=g: ktwin_ref[g, p * LANES:(p + 1) * LANES, cols].astype(BF16),
                         lambda p, g=g: vtwin_ref[g, p * LANES:(p + 1) * LANES, cols].astype(BF16),
                         None)
                    for g in range(group)], u2_ref)
        return _carry_max(carries)

    m = jnp.float32(0.0)
    for j in range(n_win):
        m = lax.cond(m > SKIP_LOG2, functools.partial(win_blocks, j), lambda m=m: m)

    def far_walk(g, _):
        stream = step * group + g

        def far_block(j):
            start = pl.multiple_of(past - (j + 1) * TK, TK)
            ck = pltpu.make_async_copy(ktc_hbm.at[stream, :, pl.ds(start, TK)], ktbuf, sem.at[0])
            cv = pltpu.make_async_copy(vtc_hbm.at[stream, :, pl.ds(start, TK)], vtbuf, sem.at[1])
            ck.start()
            cv.start()
            ck.wait()
            cv.wait()
            _sb_blocks([part(g, lambda p: ktbuf[p * LANES:(p + 1) * LANES, :].astype(BF16),
                             lambda p: vtbuf[p * LANES:(p + 1) * LANES, :].astype(BF16), None)],
                       u2_ref)

        own = [carry_scr.at[g]]
        _sb_walk(jnp.int32(n_win), n_blocks, _carry_max(own), far_block, own)
        return 0

    lax.fori_loop(0, group, far_walk, 0)
    for g in range(group):
        rows = slice(g * n_new, (g + 1) * n_new)
        _sb_finish(h_scr, rows, acc_scr.at[g], mix_scr)
        _mem_attend(h_scr, rows, n_new, mktc_ref[g].astype(BF16), mvtc_ref[g].astype(BF16), mix_scr)

    u = (_hcols(h_scr, slice(0, group * n_new), COL_CA, CONV_WIDTH)
         * _sigmoid(_hcols(h_scr, slice(0, group * n_new), COL_CB, CONV_WIDTH)))
    for g in range(group):
        base = g * (HALO + n_new)
        upad_scr[base:base + HALO, :] = cpast_ref[g]
        upad_scr[base + HALO:base + HALO + n_new, :] = u[g * n_new:(g + 1) * n_new]
        _conv_rows(h_scr, g * n_new, upad_scr, base, n_new, mix_scr, cw_ref, cb_ref, cg_ref, cbeta_ref)
        conv_ref[g] = upad_scr[base + n_new:base + n_new + HALO, :]

    y_ref[...] = _out_norm(x_ref[...], mix_scr, w_out_ref, lg_ref, lb_ref, alpha)


def _sample_call(x, ktc, vtc, cpast, mktc, mvtc, w_in, w_kvt, w_out, u2, cw, cb, cg, cbeta, lg, lb,
                 alpha, win):
    S, n_new, _ = x.shape
    past = ktc.shape[2]
    group = TK // n_new
    assert group * n_new == TK and S % group == 0 and past % win == 0 and win % TK == 0
    assert n_new % 16 == 0 and n_new >= CONV_STATE
    rows = group * n_new
    x2 = x.reshape(S * n_new, D_MODEL)
    row_spec = lambda width: pl.BlockSpec((rows, width), lambda i: (i, 0))
    grp = lambda r, width: pl.BlockSpec((group, r, width), lambda i: (i, 0, 0))
    win_spec = pl.BlockSpec((group, SB_WIDTH, win), lambda i: (i, 0, past // win - 1))
    any_spec = pl.BlockSpec(memory_space=pl.ANY)
    out_shape = (
        jax.ShapeDtypeStruct((S * n_new, D_MODEL), F32),
        jax.ShapeDtypeStruct((S * n_new, SB_WIDTH), F32),
        jax.ShapeDtypeStruct((S * n_new, SB_WIDTH), F32),
        jax.ShapeDtypeStruct((S, HALO, CONV_WIDTH), F32),
    )
    scratch = [
        pltpu.VMEM((IN_WIDTH // PROJ_COLS, rows, PROJ_COLS), F32),
        pltpu.VMEM((SB_WIDTH, TK), BF16),
        pltpu.VMEM((SB_WIDTH, TK), BF16),
        pltpu.VMEM((SB_WIDTH, TK), F32),
        pltpu.VMEM((SB_WIDTH, TK), F32),
        pltpu.SemaphoreType.DMA((2,)),
        pltpu.VMEM((group, PAIRS, 2 * n_new, LANES), BF16),
        pltpu.VMEM((group, PAIRS, 2 * n_new, LANES), F32),
        pltpu.VMEM((group, PAIRS, 2 * n_new, LANES), F32),
        pltpu.VMEM((group * (HALO + n_new), CONV_WIDTH), F32),
        pltpu.VMEM((rows, D_MODEL), BF16),
    ]
    return pl.pallas_call(
        functools.partial(_sample_kernel, group=group, n_new=n_new, win=win, past=past, alpha=alpha),
        grid=(S // group,),
        in_specs=[
            row_spec(D_MODEL), win_spec, win_spec, any_spec, any_spec,
            grp(HALO, CONV_WIDTH), grp(MEM_WIDTH, N_MEM), grp(MEM_WIDTH, N_MEM),
            _const_spec((D_MODEL, IN_WIDTH)),
            _const_spec((2 * SB_WIDTH, D_MODEL)),
            _const_spec((D_MODEL, D_MODEL)),
            _const_spec((TK, 2 * TK)),
            _const_spec((CONV_K, CONV_WIDTH)),
            _const_spec((1, CONV_WIDTH)),
            _const_spec((1, CONV_WIDTH)),
            _const_spec((1, CONV_WIDTH)),
            _const_spec((1, D_MODEL)),
            _const_spec((1, D_MODEL)),
        ],
        out_specs=(row_spec(D_MODEL), row_spec(SB_WIDTH), row_spec(SB_WIDTH), grp(HALO, CONV_WIDTH)),
        out_shape=out_shape,
        scratch_shapes=scratch,
        compiler_params=pltpu.CompilerParams(
            dimension_semantics=("arbitrary",),
            vmem_limit_bytes=VMEM_LIMIT),
        name="sample_layer",
    )(x2, ktc, vtc, ktc, vtc, cpast, mktc, mvtc, w_in, w_kvt, w_out, u2, cw, cb, cg, cbeta, lg, lb)


def _prefix_matrix():
    j = lax.broadcasted_iota(jnp.int32, (TK, TK), 0)
    s = lax.broadcasted_iota(jnp.int32, (TK, TK), 1)
    return jnp.concatenate([(j > s).astype(BF16), jnp.ones((TK, TK), BF16)], axis=1)


def _time_minor(a):
    n, time, heads, dim = a.shape
    return jnp.transpose(a, (0, 2, 3, 1)).reshape(n, heads * dim, time)


def _time_major(a, heads):
    n, width, time = a.shape
    return jnp.transpose(a.reshape(n, heads, width // heads, time), (0, 3, 1, 2))[None]


def kernel(x_prompt, x_sample, cache_sb_k, cache_sb_v, cache_conv, cache_mem_k, cache_mem_v, mem_prompt, w_in, w_mem_kv, conv_w, conv_b, conv_ln_g, conv_ln_b, w_out, ln_g, ln_b):
    depth = w_in.shape[0]
    assert depth == 1, "single layer only"
    alpha = (2 * depth) ** 0.25
    S, n_new, _ = x_sample.shape

    u2 = _prefix_matrix()
    w = w_in[0]
    w_kv = w[:, REF_K:REF_V_END]
    w_in_b = jnp.concatenate([w[:, :REF_K], w[:, REF_V_END:], w_kv], axis=1).astype(BF16)
    w_kvt = w_kv.T.astype(BF16)
    w_memt = w_mem_kv[0].T.astype(BF16)
    w_out_b = w_out[0].astype(BF16)
    cw = conv_w[0]
    cb = conv_b[0].reshape(1, CONV_WIDTH)
    cg = conv_ln_g[0].reshape(1, CONV_WIDTH)
    cbeta = conv_ln_b[0].reshape(1, CONV_WIDTH)
    lg = ln_g[0].reshape(1, D_MODEL)
    lb = ln_b[0].reshape(1, D_MODEL)

    yp, ktp, vtp, cp, mktp, mvtp = _prompt_call(
        x_prompt, mem_prompt, w_in_b, w_kvt, w_memt, w_out_b, u2, cw, cb, cg, cbeta, lg, lb,
        alpha, tm=PROMPT_TILE)

    cpast = jnp.pad(cache_conv[0], ((0, 0), (HALO_PAD, 0), (0, 0)))
    ys, ks, vs, cs = _sample_call(
        x_sample, _time_minor(cache_sb_k[0]), _time_minor(cache_sb_v[0]), cpast,
        _time_minor(cache_mem_k[0]), _time_minor(cache_mem_v[0]),
        w_in_b, w_kvt, w_out_b, u2, cw, cb, cg, cbeta, lg, lb, alpha, win=CACHE_WINDOW)

    return (
        yp,
        ys.reshape(S, n_new, D_MODEL),
        _time_major(ktp, SB_HEADS),
        _time_major(vtp, SB_HEADS),
        cp[None, :, HALO_PAD:, :],
        _time_major(mktp, MEM_HEADS),
        _time_major(mvtp, MEM_HEADS),
        ks.reshape(1, S, n_new, SB_HEADS, HEAD_DIM),
        vs.reshape(1, S, n_new, SB_HEADS, HEAD_DIM),
        cs[None, :, HALO_PAD:, :],
    )
```

```python
import functools
import math
from typing import Any, Callable, NamedTuple

import jax
import jax.numpy as jnp
from jax import lax
from jax.experimental import pallas as pl
from jax.experimental.pallas import tpu as pltpu

F32 = jnp.float32
BF16 = jnp.bfloat16

D_MODEL = 1024
SB_HEADS = 8
HEAD_DIM = 64
SB_WIDTH = SB_HEADS * HEAD_DIM
CONV_WIDTH = 256
CONV_K = 31
CONV_STATE = CONV_K - 1
MEM_HEADS = 4
MEM_WIDTH = MEM_HEADS * HEAD_DIM
N_MEM = 256
IN_WIDTH = 4 * SB_WIDTH + 3 * CONV_WIDTH + 2 * MEM_WIDTH
REF_K, REF_V_END = SB_WIDTH, 3 * SB_WIDTH
COL_Q = 0
COL_GA = SB_WIDTH
COL_CA = COL_GA + SB_WIDTH
COL_CB = COL_CA + CONV_WIDTH
COL_GC = COL_CB + CONV_WIDTH
COL_QM = COL_GC + CONV_WIDTH
COL_GM = COL_QM + MEM_WIDTH
H_WIDTH = COL_GM + MEM_WIDTH
COL_K = H_WIDTH
COL_V = COL_K + SB_WIDTH
COL_MIX_A, COL_MIX_C, COL_MIX_M = 0, SB_WIDTH, SB_WIDTH + CONV_WIDTH
LN_EPS = 1e-5
QK_SCALE = HEAD_DIM ** -0.5

LANES = 128
SUBLANES = 8
PAIRS = SB_HEADS // 2
TQ = 128
TK = 128
HALO = 32
HALO_PAD = HALO - CONV_STATE
CONV_ROWS = 256
PROJ_COLS = 256
SB_WAVE = 2
SB_SKEW = 2
TOP_ROWS = 32
N_CHUNKS = H_WIDTH // PROJ_COLS
PROJ_SPLIT = 4
PROMPT_TILE = 2 * TQ
CACHE_WINDOW = 2 * TK
INV_LN2 = 1.0 / math.log(2.0)
QK_SCALE_LOG2 = QK_SCALE * INV_LN2
SKIP_LOG2 = -152.0
VMEM_LIMIT = 60 * 1024 * 1024

NT_DIMS = (((1,), (1,)), ((), ()))
TN_NT_DIMS = (((0,), (1,)), ((), ()))


def _sigmoid(x):
    return 1.0 / (1.0 + jnp.exp(-x))


def _silu(x):
    return x * _sigmoid(x)


def _project(xb, w_refs, h_ref, chunk):
    col = chunk * PROJ_COLS
    for w_ref in w_refs:
        if col < w_ref.shape[1]:
            break
        col -= w_ref.shape[1]
    h_ref[chunk] = jnp.dot(xb, w_ref[:, col:col + PROJ_COLS], preferred_element_type=F32)


def _hcols(h_ref, rows, col, width):
    off = col % PROJ_COLS
    assert off + width <= PROJ_COLS
    return h_ref[col // PROJ_COLS, rows, off:off + width]


def _lane_lo(n_rows):
    return lax.broadcasted_iota(jnp.int32, (n_rows, LANES), 1) < HEAD_DIM


def _build_qm(h_ref, rows, qm_ref):
    n = qm_ref.shape[1] // 2
    lane_lo = _lane_lo(n)
    for p in range(PAIRS):
        qp = _hcols(h_ref, rows, COL_Q + p * LANES, LANES) * QK_SCALE_LOG2
        qm_ref[p, 0:n] = jnp.where(lane_lo, qp, 0.0).astype(BF16)
        qm_ref[p, n:2 * n] = jnp.where(lane_lo, 0.0, qp).astype(BF16)


class _Part(NamedTuple):
    qm: Callable[[int], Any]
    carry_get: Callable[[int], Any]
    carry_set: Callable[[int, Any], None]
    acc_add: Callable[[int, Any], None]
    get_kt: Callable[[int], Any]
    get_vt: Callable[[int], Any]
    mask: Any


def _whole_part(qm_ref, acc_ref, carry_ref, get_kt, get_vt, mask):
    def carry_set(p, v):
        carry_ref[p] = v

    def acc_add(p, v):
        acc_ref[p] += v

    return _Part(lambda p: qm_ref[p], lambda p: carry_ref[p], carry_set, acc_add,
                 get_kt, get_vt, mask)


def _rows_part(qm_ref, acc_ref, carry_ref, get_kt, get_vt, lo, hi):
    n = qm_ref.shape[1] // 2
    k = hi - lo
    head_a, head_b = slice(lo, hi), slice(n + lo, n + hi)

    def gather(ref, p):
        return jnp.concatenate([ref[p, head_a], ref[p, head_b]], axis=0)

    def carry_set(p, v):
        carry_ref[p, head_a] = v[:k]
        carry_ref[p, head_b] = v[k:]

    def acc_add(p, v):
        acc_ref[p, head_a] += v[:k]
        acc_ref[p, head_b] += v[k:]

    return _Part(lambda p: gather(qm_ref, p), lambda p: gather(carry_ref, p), carry_set, acc_add,
                 get_kt, get_vt, None)


def _sb_blocks(parts, u2_ref):
    chains = [(part, p) for part in parts for p in range(PAIRS)]
    waves = [chains[i:i + SB_WAVE] for i in range(0, len(chains), SB_WAVE)]

    def scores(wave):
        return [jnp.dot(part.qm(p), part.get_kt(p), preferred_element_type=F32)
                for part, p in wave]

    def prefix_sums(wave, zs):
        hits, fails = [], []
        for (part, _), z in zip(wave, zs):
            nz = -z
            soft = jnp.log(1.0 + jnp.exp2(jnp.minimum(z, nz))) * INV_LN2
            log_fail = jnp.minimum(nz, 0.0) - soft
            hits.append(log_fail + z)
            if part.mask is not None:
                log_fail = jnp.where(part.mask, log_fail, 0.0)
            fails.append(log_fail.astype(BF16))
        sums = jnp.dot(jnp.concatenate(fails, axis=0), u2_ref[...],
                       preferred_element_type=F32)
        return hits, sums

    def accumulate(wave, hits, sums):
        row = 0
        for (part, p), log_hit in zip(wave, hits):
            s = sums[row:row + log_hit.shape[0]]
            row += log_hit.shape[0]
            carry = part.carry_get(p)
            w = jnp.exp2(log_hit + s[:, :TK] + carry)
            if part.mask is not None:
                w = jnp.where(part.mask, w, 0.0)
            part.carry_set(p, carry + s[:, TK:])
            part.acc_add(p, lax.dot_general(w.astype(BF16), part.get_vt(p), NT_DIMS,
                                            preferred_element_type=F32))

    zs, summed = {}, {}
    for k in range(len(waves) + 2 * SB_SKEW):
        if k < len(waves):
            zs[k] = scores(waves[k])
        if 0 <= k - SB_SKEW < len(waves):
            summed[k - SB_SKEW] = prefix_sums(waves[k - SB_SKEW], zs.pop(k - SB_SKEW))
        if 0 <= k - 2 * SB_SKEW < len(waves):
            accumulate(waves[k - 2 * SB_SKEW], *summed.pop(k - 2 * SB_SKEW))


def _carry_max(carry_refs, lo=0, hi=None):
    n = carry_refs[0].shape[1] // 2
    hi = n if hi is None else hi
    m = None
    for ref in carry_refs:
        for p in range(PAIRS):
            for rows in (slice(lo, hi), slice(n + lo, n + hi)):
                m = ref[p, rows] if m is None else jnp.maximum(m, ref[p, rows])
    return jnp.max(m)


def _sb_walk(first, n_blocks, m0, block_fn, carry_refs):
    def cond(c):
        j, m = c
        return jnp.logical_and(j < n_blocks, m > SKIP_LOG2)

    def body(c):
        j, _ = c
        block_fn(j)
        return j + 1, _carry_max(carry_refs)

    return lax.while_loop(cond, body, (first, m0))


def _sb_finish(h_ref, rows, acc_ref, mix_ref):
    n = acc_ref.shape[1] // 2
    lane_lo = _lane_lo(n)
    for p in range(PAIRS):
        o = jnp.where(lane_lo, acc_ref[p, 0:n], acc_ref[p, n:2 * n])
        g = _hcols(h_ref, rows, COL_GA + p * LANES, LANES)
        mix_ref[rows, COL_MIX_A + p * LANES:COL_MIX_A + (p + 1) * LANES] = (
            o * _silu(g)).astype(BF16)


def _stacked_index(n_rows):
    row = lax.broadcasted_iota(jnp.int32, (2 * n_rows, TK), 0)
    col = lax.broadcasted_iota(jnp.int32, (2 * n_rows, TK), 1)
    return jnp.where(row >= n_rows, row - n_rows, row), col


def _conv_rows(h_ref, row, upad_ref, urow, nr, mix_ref, cw_ref, cb_ref, g_ref, b_ref):
    c = jnp.broadcast_to(cb_ref[...], (nr, CONV_WIDTH))
    for s in range(SUBLANES):
        taps = [i for i in range(CONV_K) if (HALO_PAD + i) % SUBLANES == s]
        span = nr if s == 0 else nr + SUBLANES
        part = None
        for i in taps:
            term = cw_ref[i:i + 1, :] * upad_ref[pl.ds(urow + (HALO_PAD + i - s), span), :]
            part = term if part is None else part + term
        c = c + part[s:s + nr]
    mu = jnp.mean(c, axis=-1, keepdims=True)
    d = c - mu
    var = jnp.mean(d * d, axis=-1, keepdims=True)
    n = d * lax.rsqrt(var + LN_EPS) * g_ref[...] + b_ref[...]
    gate = _hcols(h_ref, pl.ds(row, nr), COL_GC, CONV_WIDTH)
    mix_ref[pl.ds(row, nr), COL_MIX_C:COL_MIX_C + CONV_WIDTH] = (_silu(n) * _silu(gate)).astype(BF16)


def _mem_attend(h_ref, rows, n_rows, mkt, mvt, mix_ref):
    lane_lo = _lane_lo(n_rows)
    for p in range(MEM_HEADS // 2):
        qp = _hcols(h_ref, rows, COL_QM + p * LANES, LANES) * QK_SCALE
        mkp = mkt[p * LANES:(p + 1) * LANES, :]
        mvp = mvt[p * LANES:(p + 1) * LANES, :]
        qh = jnp.concatenate([jnp.where(lane_lo, qp, 0.0), jnp.where(lane_lo, 0.0, qp)],
                             axis=0).astype(BF16)
        s = jnp.dot(qh, mkp, preferred_element_type=F32)
        e = jnp.exp(s - jnp.max(s, axis=-1, keepdims=True))
        o = lax.dot_general(e.astype(BF16), mvp, NT_DIMS, preferred_element_type=F32)
        o = o / jnp.sum(e, axis=-1, keepdims=True)
        o = jnp.where(lane_lo, o[:n_rows], o[n_rows:])
        g = _hcols(h_ref, rows, COL_GM + p * LANES, LANES)
        mix_ref[rows, COL_MIX_M + p * LANES:COL_MIX_M + (p + 1) * LANES] = (o * _silu(g)).astype(BF16)


def _out_norm(x, mix_ref, w_out_ref, g_ref, b_ref, alpha):
    r = alpha * x + jnp.dot(mix_ref[...], w_out_ref[...], preferred_element_type=F32)
    mu = jnp.mean(r, axis=-1, keepdims=True)
    d = r - mu
    var = jnp.mean(d * d, axis=-1, keepdims=True)
    return d * lax.rsqrt(var + LN_EPS) * g_ref[...] + b_ref[...]


def _prompt_kernel(x_ref, xn_ref, mem_ref, w_q_ref, w_rest_ref, w_kvt_ref, w_memt_ref, w_out_ref,
                   u2_ref, cw_ref, cb_ref,
                   cg_ref, cbeta_ref, lg_ref, lb_ref,
                   y_ref, kt_ref, vt_ref, conv_ref, mkt_ref, mvt_ref,
                   h_scr, kt_scr, vt_scr, mkt_scr, mvt_scr, qm_scr, acc_scr, carry_scr,
                   upad_scr, mix_scr, *, tm, alpha):
    t = pl.program_id(1)
    n_sub = tm // TQ
    assert tm % CONV_ROWS == 0
    w_in_refs = (w_q_ref, w_rest_ref)
    row, col = _stacked_index(TQ)
    causal = col < row

    @pl.when(t == 0)
    def _():
        kvt = lax.dot_general(w_memt_ref[...], mem_ref[0].astype(BF16), NT_DIMS,
                              preferred_element_type=F32)
        mkt_ref[0] = kvt[:MEM_WIDTH]
        mvt_ref[0] = kvt[MEM_WIDTH:]
        mkt_scr[...] = kvt[:MEM_WIDTH].astype(BF16)
        mvt_scr[...] = kvt[MEM_WIDTH:].astype(BF16)
        upad_scr[0:HALO, :] = jnp.zeros((HALO, CONV_WIDTH), F32)

    h_cur = h_scr.at[t % 2]
    h_next = h_scr.at[(t + 1) % 2]
    xb = x_ref[0].astype(BF16)

    @pl.when(t == 0)
    def _():
        for c in range(N_CHUNKS):
            _project(xb, w_in_refs, h_scr.at[0], c)

    kvt = lax.dot_general(w_kvt_ref[...], xb, TN_NT_DIMS, preferred_element_type=F32)
    kt_ref[0] = kvt[:SB_WIDTH]
    vt_ref[0] = kvt[SB_WIDTH:]
    blk0 = t * n_sub
    for c in range(n_sub):
        kt_scr[blk0 + c] = kvt[:SB_WIDTH, c * TK:(c + 1) * TK].astype(BF16)
        vt_scr[blk0 + c] = kvt[SB_WIDTH:, c * TK:(c + 1) * TK].astype(BF16)

    for i in range(n_sub):
        _build_qm(h_cur, slice(i * TQ, (i + 1) * TQ), qm_scr.at[i])
    acc_scr[...] = jnp.zeros(acc_scr.shape, F32)
    carry_scr[...] = jnp.zeros(carry_scr.shape, F32)
    carries = [carry_scr.at[i] for i in range(n_sub)]

    def refs(i, blk):
        return (qm_scr.at[i], acc_scr.at[i], carry_scr.at[i],
                lambda p: kt_scr[blk, p * LANES:(p + 1) * LANES, :],
                lambda p: vt_scr[blk, p * LANES:(p + 1) * LANES, :])

    def whole(j, subs=range(n_sub)):
        return [_whole_part(*refs(i, blk0 + i - 1 - j), None) for i in subs]

    def some_rows(j, lo, hi):
        return [_rows_part(*refs(i, blk0 + i - 1 - j), lo, hi) for i in range(n_sub)]

    def left():
        return _carry_max(carries, 0, TOP_ROWS), _carry_max(carries, TOP_ROWS, TQ)

    xb_next = xn_ref[0].astype(BF16)
    for c in range(PROJ_SPLIT):
        _project(xb_next, w_in_refs, h_next, c)
    _sb_blocks([_whole_part(*refs(i, blk0 + i), causal) for i in range(n_sub)], u2_ref)

    def speculate():
        parts = [part for both in zip(whole(0), some_rows(1, 0, TOP_ROWS)) for part in both]
        _sb_blocks(parts, u2_ref)
        return left()

    def all_left():
        return jnp.float32(0.0), jnp.float32(0.0)

    ahead = blk0 >= 2
    m_top, m_bot = lax.cond(ahead, speculate, all_left)

    def catch_up():
        _sb_blocks(some_rows(1, TOP_ROWS, TQ), u2_ref)
        return _carry_max(carries, TOP_ROWS, TQ)

    m_bot = lax.cond(jnp.logical_and(ahead, m_bot > SKIP_LOG2), catch_up, lambda: m_bot)

    def trip_cond(c):
        j, top, bot = c
        return jnp.logical_and(j < blk0, jnp.maximum(top, bot) > SKIP_LOG2)

    def trip(c):
        j, _, bot = c

        def every_row():
            _sb_blocks(whole(j), u2_ref)
            return left()

        def first_rows():
            _sb_blocks(some_rows(j, 0, TOP_ROWS), u2_ref)
            return _carry_max(carries, 0, TOP_ROWS), bot

        return (j + 1,) + lax.cond(bot > SKIP_LOG2, every_row, first_rows)

    first_trip = jnp.where(ahead, 2, 0).astype(jnp.int32)
    j, m_top, m_bot = lax.while_loop(trip_cond, trip, (first_trip, m_top, m_bot))
    for d in range(n_sub - 1):
        def tail(d=d):
            _sb_blocks(whole(blk0 + d, range(d + 1, n_sub)), u2_ref)
            return left()
        m_top, m_bot = lax.cond(
            jnp.logical_and(j >= blk0, jnp.maximum(m_top, m_bot) > SKIP_LOG2),
            tail, lambda: (m_top, m_bot))
    for c in range(PROJ_SPLIT, N_CHUNKS):
        _project(xb_next, w_in_refs, h_next, c)
    for i in range(n_sub):
        _sb_finish(h_cur, slice(i * TQ, (i + 1) * TQ), acc_scr.at[i], mix_scr)

    u = (_hcols(h_cur, slice(0, tm), COL_CA, CONV_WIDTH)
         * _sigmoid(_hcols(h_cur, slice(0, tm), COL_CB, CONV_WIDTH)))
    upad_scr[HALO:HALO + tm, :] = u

    for r in range(0, tm, CONV_ROWS):
        _conv_rows(h_cur, r, upad_scr, r, CONV_ROWS, mix_scr, cw_ref, cb_ref, cg_ref, cbeta_ref)
    last_rows = upad_scr[tm:tm + HALO, :]
    upad_scr[0:HALO, :] = last_rows
    conv_ref[0] = last_rows

    _mem_attend(h_cur, slice(0, tm), tm, mkt_scr[...], mvt_scr[...], mix_scr)
    y_ref[0] = _out_norm(x_ref[0], mix_scr, w_out_ref, lg_ref, lb_ref, alpha)


def _const_spec(shape):
    return pl.BlockSpec(shape, lambda *_: (0,) * len(shape), pipeline_mode=pl.Buffered(1))


def _prompt_call(x, mem, w_q, w_rest, w_kvt, w_memt, w_out, u2, cw, cb, cg, cbeta, lg, lb, alpha, tm):
    B, T, _ = x.shape
    assert T % tm == 0 and tm % TQ == 0 and TQ == TK
    grid = (B, T // tm)
    row_spec = lambda width: pl.BlockSpec((1, tm, width), lambda b, t: (b, t, 0))
    col_spec = lambda height: pl.BlockSpec((1, height, tm), lambda b, t: (b, 0, t))
    per_b = lambda r, width: pl.BlockSpec((1, r, width), lambda b, t: (b, 0, 0))
    out_shape = (
        jax.ShapeDtypeStruct((B, T, D_MODEL), F32),
        jax.ShapeDtypeStruct((B, SB_WIDTH, T), F32),
        jax.ShapeDtypeStruct((B, SB_WIDTH, T), F32),
        jax.ShapeDtypeStruct((B, HALO, CONV_WIDTH), F32),
        jax.ShapeDtypeStruct((B, MEM_WIDTH, N_MEM), F32),
        jax.ShapeDtypeStruct((B, MEM_WIDTH, N_MEM), F32),
    )
    scratch = [
        pltpu.VMEM((2, N_CHUNKS, tm, PROJ_COLS), F32),
        pltpu.VMEM((T // TK, SB_WIDTH, TK), BF16),
        pltpu.VMEM((T // TK, SB_WIDTH, TK), BF16),
        pltpu.VMEM((MEM_WIDTH, N_MEM), BF16),
        pltpu.VMEM((MEM_WIDTH, N_MEM), BF16),
        pltpu.VMEM((tm // TQ, PAIRS, 2 * TQ, LANES), BF16),
        pltpu.VMEM((tm // TQ, PAIRS, 2 * TQ, LANES), F32),
        pltpu.VMEM((tm // TQ, PAIRS, 2 * TQ, LANES), F32),
        pltpu.VMEM((HALO + tm, CONV_WIDTH), F32),
        pltpu.VMEM((tm, D_MODEL), BF16),
    ]
    return pl.pallas_call(
        functools.partial(_prompt_kernel, tm=tm, alpha=alpha),
        grid=grid,
        in_specs=[
            row_spec(D_MODEL),
            pl.BlockSpec((1, tm, D_MODEL), lambda b, t: (b, jnp.minimum(t + 1, T // tm - 1), 0)),
            per_b(N_MEM, D_MODEL),
            _const_spec((D_MODEL, SB_WIDTH)),
            _const_spec((D_MODEL, H_WIDTH - SB_WIDTH)),
            _const_spec((2 * SB_WIDTH, D_MODEL)),
            _const_spec((2 * MEM_WIDTH, D_MODEL)),
            _const_spec((D_MODEL, D_MODEL)),
            _const_spec((TK, 2 * TK)),
            _const_spec((CONV_K, CONV_WIDTH)),
            _const_spec((1, CONV_WIDTH)),
            _const_spec((1, CONV_WIDTH)),
            _const_spec((1, CONV_WIDTH)),
            _const_spec((1, D_MODEL)),
            _const_spec((1, D_MODEL)),
        ],
        out_specs=(
            row_spec(D_MODEL), col_spec(SB_WIDTH), col_spec(SB_WIDTH),
            per_b(HALO, CONV_WIDTH), per_b(MEM_WIDTH, N_MEM), per_b(MEM_WIDTH, N_MEM),
        ),
        out_shape=out_shape,
        scratch_shapes=scratch,
        compiler_params=pltpu.CompilerParams(
            dimension_semantics=("arbitrary", "arbitrary"),
            vmem_limit_bytes=VMEM_LIMIT),
        name="prompt_layer",
    )(x, x, mem, w_q, w_rest, w_kvt, w_memt, w_out, u2, cw, cb, cg, cbeta, lg, lb)


def _sample_kernel(x_ref, ktwin_ref, vtwin_ref, ktc_hbm, vtc_hbm, cpast_ref, mktc_ref, mvtc_ref,
                   w_q_ref, w_rest_ref, w_kv_ref, w_kvt_ref, w_out_ref, u2_ref, cw_ref, cb_ref, cg_ref,
                   cbeta_ref,
                   lg_ref, lb_ref,
                   y_ref, k_ref, v_ref, conv_ref,
                   h_scr, ktnew_scr, vtnew_scr, ktbuf, vtbuf, sem, qm_scr, acc_scr, carry_scr,
                   upad_scr, mix_scr, *, group, n_new, win, past, alpha):
    step = pl.program_id(0)
    row, col = _stacked_index(n_new)
    n_win = win // TK
    n_blocks = past // TK

    xb = x_ref[...].astype(BF16)
    for c in range(IN_WIDTH // PROJ_COLS):
        _project(xb, (w_q_ref, w_rest_ref, w_kv_ref), h_scr, c)
    for c in range(SB_WIDTH // PROJ_COLS):
        cols = slice(c * PROJ_COLS, (c + 1) * PROJ_COLS)
        k_ref[:, cols] = h_scr[COL_K // PROJ_COLS + c]
        v_ref[:, cols] = h_scr[COL_V // PROJ_COLS + c]
    kvt = lax.dot_general(w_kvt_ref[...], xb, TN_NT_DIMS, preferred_element_type=F32)
    ktnew_scr[...] = kvt[:SB_WIDTH].astype(BF16)
    vtnew_scr[...] = kvt[SB_WIDTH:].astype(BF16)

    for g in range(group):
        _build_qm(h_scr, slice(g * n_new, (g + 1) * n_new), qm_scr.at[g])
    acc_scr[...] = jnp.zeros(acc_scr.shape, F32)
    carry_scr[...] = jnp.zeros(carry_scr.shape, F32)
    carries = [carry_scr.at[g] for g in range(group)]

    def part(g, get_kt, get_vt, mask):
        return _whole_part(qm_scr.at[g], acc_scr.at[g], carry_scr.at[g], get_kt, get_vt, mask)

    def own_past(g):
        rel = col - g * n_new
        return jnp.logical_and(rel >= 0, rel < row)

    _sb_blocks([part(g, lambda p: ktnew_scr[p * LANES:(p + 1) * LANES, :],
                     lambda p: vtnew_scr[p * LANES:(p + 1) * LANES, :], own_past(g))
                for g in range(group)], u2_ref)

    def win_blocks(j):
        cols = slice(win - (j + 1) * TK, win - j * TK)
        _sb_blocks([part(g, lambda p, g=g: ktwin_ref[g, p * LANES:(p + 1) * LANES, cols].astype(BF16),
                         lambda p, g=g: vtwin_ref[g, p * LANES:(p + 1) * LANES, cols].astype(BF16),
                         None)
                    for g in range(group)], u2_ref)
        return _carry_max(carries)

    m = jnp.float32(0.0)
    for j in range(n_win):
        m = lax.cond(m > SKIP_LOG2, functools.partial(win_blocks, j), lambda m=m: m)

    def far_walk(g, _):
        stream = step * group + g

        def far_block(j):
            start = pl.multiple_of(past - (j + 1) * TK, TK)
            ck = pltpu.make_async_copy(ktc_hbm.at[stream, :, pl.ds(start, TK)], ktbuf, sem.at[0])
            cv = pltpu.make_async_copy(vtc_hbm.at[stream, :, pl.ds(start, TK)], vtbuf, sem.at[1])
            ck.start()
            cv.start()
            ck.wait()
            cv.wait()
            _sb_blocks([part(g, lambda p: ktbuf[p * LANES:(p + 1) * LANES, :].astype(BF16),
                             lambda p: vtbuf[p * LANES:(p + 1) * LANES, :].astype(BF16), None)],
                       u2_ref)

        own = [carry_scr.at[g]]
        _sb_walk(jnp.int32(n_win), n_blocks, _carry_max(own), far_block, own)
        return 0

    lax.fori_loop(0, group, far_walk, 0)
    for g in range(group):
        rows = slice(g * n_new, (g + 1) * n_new)
        _sb_finish(h_scr, rows, acc_scr.at[g], mix_scr)
        _mem_attend(h_scr, rows, n_new, mktc_ref[g].astype(BF16), mvtc_ref[g].astype(BF16), mix_scr)

    u = (_hcols(h_scr, slice(0, group * n_new), COL_CA, CONV_WIDTH)
         * _sigmoid(_hcols(h_scr, slice(0, group * n_new), COL_CB, CONV_WIDTH)))
    for g in range(group):
        base = g * (HALO + n_new)
        upad_scr[base:base + HALO, :] = cpast_ref[g]
        upad_scr[base + HALO:base + HALO + n_new, :] = u[g * n_new:(g + 1) * n_new]
        _conv_rows(h_scr, g * n_new, upad_scr, base, n_new, mix_scr, cw_ref, cb_ref, cg_ref, cbeta_ref)
        conv_ref[g] = upad_scr[base + n_new:base + n_new + HALO, :]

    y_ref[...] = _out_norm(x_ref[...], mix_scr, w_out_ref, lg_ref, lb_ref, alpha)


def _sample_call(x, ktc, vtc, cpast, mktc, mvtc, w_q, w_rest, w_kv, w_kvt, w_out, u2, cw, cb, cg, cbeta,
                 lg, lb,
                 alpha, win):
    S, n_new, _ = x.shape
    past = ktc.shape[2]
    group = TK // n_new
    assert group * n_new == TK and S % group == 0 and past % win == 0 and win % TK == 0
    assert n_new % 16 == 0 and n_new >= CONV_STATE
    rows = group * n_new
    x2 = x.reshape(S * n_new, D_MODEL)
    row_spec = lambda width: pl.BlockSpec((rows, width), lambda i: (i, 0))
    grp = lambda r, width: pl.BlockSpec((group, r, width), lambda i: (i, 0, 0))
    win_spec = pl.BlockSpec((group, SB_WIDTH, win), lambda i: (i, 0, past // win - 1))
    any_spec = pl.BlockSpec(memory_space=pl.ANY)
    out_shape = (
        jax.ShapeDtypeStruct((S * n_new, D_MODEL), F32),
        jax.ShapeDtypeStruct((S * n_new, SB_WIDTH), F32),
        jax.ShapeDtypeStruct((S * n_new, SB_WIDTH), F32),
        jax.ShapeDtypeStruct((S, HALO, CONV_WIDTH), F32),
    )
    scratch = [
        pltpu.VMEM((IN_WIDTH // PROJ_COLS, rows, PROJ_COLS), F32),
        pltpu.VMEM((SB_WIDTH, TK), BF16),
        pltpu.VMEM((SB_WIDTH, TK), BF16),
        pltpu.VMEM((SB_WIDTH, TK), F32),
        pltpu.VMEM((SB_WIDTH, TK), F32),
        pltpu.SemaphoreType.DMA((2,)),
        pltpu.VMEM((group, PAIRS, 2 * n_new, LANES), BF16),
        pltpu.VMEM((group, PAIRS, 2 * n_new, LANES), F32),
        pltpu.VMEM((group, PAIRS, 2 * n_new, LANES), F32),
        pltpu.VMEM((group * (HALO + n_new), CONV_WIDTH), F32),
        pltpu.VMEM((rows, D_MODEL), BF16),
    ]
    return pl.pallas_call(
        functools.partial(_sample_kernel, group=group, n_new=n_new, win=win, past=past, alpha=alpha),
        grid=(S // group,),
        in_specs=[
            row_spec(D_MODEL), win_spec, win_spec, any_spec, any_spec,
            grp(HALO, CONV_WIDTH), grp(MEM_WIDTH, N_MEM), grp(MEM_WIDTH, N_MEM),
            _const_spec((D_MODEL, SB_WIDTH)),
            _const_spec((D_MODEL, H_WIDTH - SB_WIDTH)),
            _const_spec((D_MODEL, 2 * SB_WIDTH)),
            _const_spec((2 * SB_WIDTH, D_MODEL)),
            _const_spec((D_MODEL, D_MODEL)),
            _const_spec((TK, 2 * TK)),
            _const_spec((CONV_K, CONV_WIDTH)),
            _const_spec((1, CONV_WIDTH)),
            _const_spec((1, CONV_WIDTH)),
            _const_spec((1, CONV_WIDTH)),
            _const_spec((1, D_MODEL)),
            _const_spec((1, D_MODEL)),
        ],
        out_specs=(row_spec(D_MODEL), row_spec(SB_WIDTH), row_spec(SB_WIDTH), grp(HALO, CONV_WIDTH)),
        out_shape=out_shape,
        scratch_shapes=scratch,
        compiler_params=pltpu.CompilerParams(
            dimension_semantics=("arbitrary",),
            vmem_limit_bytes=VMEM_LIMIT),
        name="sample_layer",
    )(x2, ktc, vtc, ktc, vtc, cpast, mktc, mvtc, w_q, w_rest, w_kv, w_kvt, w_out, u2, cw, cb, cg,
      cbeta, lg, lb)


def _prefix_matrix():
    j = lax.broadcasted_iota(jnp.int32, (TK, TK), 0)
    s = lax.broadcasted_iota(jnp.int32, (TK, TK), 1)
    return jnp.concatenate([(j > s).astype(BF16), jnp.ones((TK, TK), BF16)], axis=1)


def _time_minor(a):
    n, time, heads, dim = a.shape
    return jnp.transpose(a, (0, 2, 3, 1)).reshape(n, heads * dim, time)


def _time_major(a, heads):
    n, width, time = a.shape
    return jnp.transpose(a.reshape(n, heads, width // heads, time), (0, 3, 1, 2))[None]


def kernel(x_prompt, x_sample, cache_sb_k, cache_sb_v, cache_conv, cache_mem_k, cache_mem_v, mem_prompt, w_in, w_mem_kv, conv_w, conv_b, conv_ln_g, conv_ln_b, w_out, ln_g, ln_b):
    depth = w_in.shape[0]
    assert depth == 1, "single layer only"
    alpha = (2 * depth) ** 0.25
    S, n_new, _ = x_sample.shape

    u2 = _prefix_matrix()
    w = w_in[0]
    w_q = w[:, :REF_K].astype(BF16)
    w_rest = w[:, REF_V_END:].astype(BF16)
    w_kv = w[:, REF_K:REF_V_END].astype(BF16)
    w_kvt = w_kv
    w_memt = w_mem_kv[0].T.astype(BF16)
    w_out_b = w_out[0].astype(BF16)
    cw = conv_w[0]
    cb = conv_b[0].reshape(1, CONV_WIDTH)
    cg = conv_ln_g[0].reshape(1, CONV_WIDTH)
    cbeta = conv_ln_b[0].reshape(1, CONV_WIDTH)
    lg = ln_g[0].reshape(1, D_MODEL)
    lb = ln_b[0].reshape(1, D_MODEL)

    yp, ktp, vtp, cp, mktp, mvtp = _prompt_call(
        x_prompt, mem_prompt, w_q, w_rest, w_kvt, w_memt, w_out_b, u2, cw, cb, cg, cbeta, lg, lb,
        alpha, tm=PROMPT_TILE)

    cpast = jnp.pad(cache_conv[0], ((0, 0), (HALO_PAD, 0), (0, 0)))
    ys, ks, vs, cs = _sample_call(
        x_sample, _time_minor(cache_sb_k[0]), _time_minor(cache_sb_v[0]), cpast,
        _time_minor(cache_mem_k[0]), _time_minor(cache_mem_v[0]),
        w_q, w_rest, w_kv, w_kvt, w_out_b, u2, cw, cb, cg, cbeta, lg, lb, alpha, win=CACHE_WINDOW)

    return (
        yp,
        ys.reshape(S, n_new, D_MODEL),
        _time_major(ktp, SB_HEADS),
        _time_major(vtp, SB_HEADS),
        cp[None, :, HALO_PAD:, :],
        _time_major(mktp, MEM_HEADS),
        _time_major(mvtp, MEM_HEADS),
        ks.reshape(1, S, n_new, SB_HEADS, HEAD_DIM),
        vs.reshape(1, S, n_new, SB_HEADS, HEAD_DIM),
        cs[None, :, HALO_PAD:, :],
    )
```

```python
import functools
import math
from typing import Any, Callable, NamedTuple

import jax
import jax.numpy as jnp
from jax import lax
from jax.experimental import pallas as pl
from jax.experimental.pallas import tpu as pltpu

F32 = jnp.float32
BF16 = jnp.bfloat16

D_MODEL = 1024
SB_HEADS = 8
HEAD_DIM = 64
SB_WIDTH = SB_HEADS * HEAD_DIM
CONV_WIDTH = 256
CONV_K = 31
CONV_STATE = CONV_K - 1
MEM_HEADS = 4
MEM_WIDTH = MEM_HEADS * HEAD_DIM
N_MEM = 256
IN_WIDTH = 4 * SB_WIDTH + 3 * CONV_WIDTH + 2 * MEM_WIDTH
REF_K, REF_V_END = SB_WIDTH, 3 * SB_WIDTH
COL_Q = 0
COL_GA = SB_WIDTH
COL_CA = COL_GA + SB_WIDTH
COL_CB = COL_CA + CONV_WIDTH
COL_GC = COL_CB + CONV_WIDTH
COL_QM = COL_GC + CONV_WIDTH
COL_GM = COL_QM + MEM_WIDTH
H_WIDTH = COL_GM + MEM_WIDTH
COL_K = H_WIDTH
COL_V = COL_K + SB_WIDTH
COL_MIX_A, COL_MIX_C, COL_MIX_M = 0, SB_WIDTH, SB_WIDTH + CONV_WIDTH
LN_EPS = 1e-5
QK_SCALE = HEAD_DIM ** -0.5

LANES = 128
SUBLANES = 8
PAIRS = SB_HEADS // 2
TQ = 128
TK = 128
HALO = 32
HALO_PAD = HALO - CONV_STATE
CONV_ROWS = 256
PROJ_COLS = 256
SB_WAVE = 2
SB_SKEW = 2
TOP_ROWS = 32
N_CHUNKS = H_WIDTH // PROJ_COLS
PROJ_SPLIT = 4
PROMPT_TILE = 2 * TQ
CACHE_WINDOW = 2 * TK
INV_LN2 = 1.0 / math.log(2.0)
QK_SCALE_LOG2 = QK_SCALE * INV_LN2
SKIP_LOG2 = -152.0
VMEM_LIMIT = 60 * 1024 * 1024

NT_DIMS = (((1,), (1,)), ((), ()))
TN_NT_DIMS = (((0,), (1,)), ((), ()))


def _sigmoid(x):
    return 1.0 / (1.0 + jnp.exp(-x))


def _silu(x):
    return x * _sigmoid(x)


def _project(xb, w_refs, h_ref, chunk):
    col = chunk * PROJ_COLS
    for w_ref in w_refs:
        if col < w_ref.shape[1]:
            break
        col -= w_ref.shape[1]
    h_ref[chunk] = jnp.dot(xb, w_ref[:, col:col + PROJ_COLS], preferred_element_type=F32)


def _hcols(h_ref, rows, col, width):
    off = col % PROJ_COLS
    assert off + width <= PROJ_COLS
    return h_ref[col // PROJ_COLS, rows, off:off + width]


def _lane_lo(n_rows):
    return lax.broadcasted_iota(jnp.int32, (n_rows, LANES), 1) < HEAD_DIM


def _build_qm(h_ref, rows, qm_ref):
    n = qm_ref.shape[1] // 2
    lane_lo = _lane_lo(n)
    for p in range(PAIRS):
        qp = _hcols(h_ref, rows, COL_Q + p * LANES, LANES) * QK_SCALE_LOG2
        qm_ref[p, 0:n] = jnp.where(lane_lo, qp, 0.0).astype(BF16)
        qm_ref[p, n:2 * n] = jnp.where(lane_lo, 0.0, qp).astype(BF16)


class _Part(NamedTuple):
    qm: Callable[[int], Any]
    carry_get: Callable[[int], Any]
    carry_set: Callable[[int, Any], None]
    acc_add: Callable[[int, Any], None]
    get_kt: Callable[[int], Any]
    get_vt: Callable[[int], Any]
    mask: Any


def _whole_part(qm_ref, acc_ref, carry_ref, get_kt, get_vt, mask):
    def carry_set(p, v):
        carry_ref[p] = v

    def acc_add(p, v):
        acc_ref[p] += v

    return _Part(lambda p: qm_ref[p], lambda p: carry_ref[p], carry_set, acc_add,
                 get_kt, get_vt, mask)


def _rows_part(qm_ref, acc_ref, carry_ref, get_kt, get_vt, lo, hi):
    n = qm_ref.shape[1] // 2
    k = hi - lo
    head_a, head_b = slice(lo, hi), slice(n + lo, n + hi)

    def gather(ref, p):
        return jnp.concatenate([ref[p, head_a], ref[p, head_b]], axis=0)

    def carry_set(p, v):
        carry_ref[p, head_a] = v[:k]
        carry_ref[p, head_b] = v[k:]

    def acc_add(p, v):
        acc_ref[p, head_a] += v[:k]
        acc_ref[p, head_b] += v[k:]

    return _Part(lambda p: gather(qm_ref, p), lambda p: gather(carry_ref, p), carry_set, acc_add,
                 get_kt, get_vt, None)


def _sb_blocks(parts, u2_ref):
    chains = [(part, p) for part in parts for p in range(PAIRS)]
    waves = [chains[i:i + SB_WAVE] for i in range(0, len(chains), SB_WAVE)]

    def scores(wave):
        return [jnp.dot(part.qm(p), part.get_kt(p), preferred_element_type=F32)
                for part, p in wave]

    def prefix_sums(wave, zs):
        hits, fails = [], []
        for (part, _), z in zip(wave, zs):
            nz = -z
            soft = jnp.log(1.0 + jnp.exp2(jnp.minimum(z, nz))) * INV_LN2
            log_fail = jnp.minimum(nz, 0.0) - soft
            hits.append(log_fail + z)
            if part.mask is not None:
                log_fail = jnp.where(part.mask, log_fail, 0.0)
            fails.append(log_fail.astype(BF16))
        sums = jnp.dot(jnp.concatenate(fails, axis=0), u2_ref[...],
                       preferred_element_type=F32)
        return hits, sums

    def accumulate(wave, hits, sums):
        row = 0
        for (part, p), log_hit in zip(wave, hits):
            s = sums[row:row + log_hit.shape[0]]
            row += log_hit.shape[0]
            carry = part.carry_get(p)
            w = jnp.exp2(log_hit + s[:, :TK] + carry)
            if part.mask is not None:
                w = jnp.where(part.mask, w, 0.0)
            part.carry_set(p, carry + s[:, TK:])
            part.acc_add(p, lax.dot_general(w.astype(BF16), part.get_vt(p), NT_DIMS,
                                            preferred_element_type=F32))

    zs, summed = {}, {}
    for k in range(len(waves) + 2 * SB_SKEW):
        if k < len(waves):
            zs[k] = scores(waves[k])
        if 0 <= k - SB_SKEW < len(waves):
            summed[k - SB_SKEW] = prefix_sums(waves[k - SB_SKEW], zs.pop(k - SB_SKEW))
        if 0 <= k - 2 * SB_SKEW < len(waves):
            accumulate(waves[k - 2 * SB_SKEW], *summed.pop(k - 2 * SB_SKEW))


def _carry_max(carry_refs, lo=0, hi=None):
    n = carry_refs[0].shape[1] // 2
    hi = n if hi is None else hi
    m = None
    for ref in carry_refs:
        for p in range(PAIRS):
            for rows in (slice(lo, hi), slice(n + lo, n + hi)):
                m = ref[p, rows] if m is None else jnp.maximum(m, ref[p, rows])
    return jnp.max(m)


def _sb_walk(first, n_blocks, m0, block_fn, carry_refs):
    def cond(c):
        j, m = c
        return jnp.logical_and(j < n_blocks, m > SKIP_LOG2)

    def body(c):
        j, _ = c
        block_fn(j)
        return j + 1, _carry_max(carry_refs)

    return lax.while_loop(cond, body, (first, m0))


def _sb_finish(h_ref, rows, acc_ref, mix_ref):
    n = acc_ref.shape[1] // 2
    lane_lo = _lane_lo(n)
    for p in range(PAIRS):
        o = jnp.where(lane_lo, acc_ref[p, 0:n], acc_ref[p, n:2 * n])
        g = _hcols(h_ref, rows, COL_GA + p * LANES, LANES)
        mix_ref[rows, COL_MIX_A + p * LANES:COL_MIX_A + (p + 1) * LANES] = (
            o * _silu(g)).astype(BF16)


def _stacked_index(n_rows):
    row = lax.broadcasted_iota(jnp.int32, (2 * n_rows, TK), 0)
    col = lax.broadcasted_iota(jnp.int32, (2 * n_rows, TK), 1)
    return jnp.where(row >= n_rows, row - n_rows, row), col


def _conv_rows(h_ref, row, upad_ref, urow, nr, mix_ref, cw_ref, cb_ref, g_ref, b_ref):
    c = jnp.broadcast_to(cb_ref[...], (nr, CONV_WIDTH))
    for s in range(SUBLANES):
        taps = [i for i in range(CONV_K) if (HALO_PAD + i) % SUBLANES == s]
        span = nr if s == 0 else nr + SUBLANES
        part = None
        for i in taps:
            term = cw_ref[i:i + 1, :] * upad_ref[pl.ds(urow + (HALO_PAD + i - s), span), :]
            part = term if part is None else part + term
        c = c + part[s:s + nr]
    mu = jnp.mean(c, axis=-1, keepdims=True)
    d = c - mu
    var = jnp.mean(d * d, axis=-1, keepdims=True)
    n = d * lax.rsqrt(var + LN_EPS) * g_ref[...] + b_ref[...]
    gate = _hcols(h_ref, pl.ds(row, nr), COL_GC, CONV_WIDTH)
    mix_ref[pl.ds(row, nr), COL_MIX_C:COL_MIX_C + CONV_WIDTH] = (_silu(n) * _silu(gate)).astype(BF16)


def _mem_attend(h_ref, rows, n_rows, mkt, mvt, mix_ref):
    lane_lo = _lane_lo(n_rows)
    for p in range(MEM_HEADS // 2):
        qp = _hcols(h_ref, rows, COL_QM + p * LANES, LANES) * QK_SCALE
        mkp = mkt[p * LANES:(p + 1) * LANES, :]
        mvp = mvt[p * LANES:(p + 1) * LANES, :]
        qh = jnp.concatenate([jnp.where(lane_lo, qp, 0.0), jnp.where(lane_lo, 0.0, qp)],
                             axis=0).astype(BF16)
        s = jnp.dot(qh, mkp, preferred_element_type=F32)
        e = jnp.exp(s - jnp.max(s, axis=-1, keepdims=True))
        o = lax.dot_general(e.astype(BF16), mvp, NT_DIMS, preferred_element_type=F32)
        o = o / jnp.sum(e, axis=-1, keepdims=True)
        o = jnp.where(lane_lo, o[:n_rows], o[n_rows:])
        g = _hcols(h_ref, rows, COL_GM + p * LANES, LANES)
        mix_ref[rows, COL_MIX_M + p * LANES:COL_MIX_M + (p + 1) * LANES] = (o * _silu(g)).astype(BF16)


def _out_norm(x, mix_ref, w_out_ref, g_ref, b_ref, alpha):
    r = alpha * x + jnp.dot(mix_ref[...], w_out_ref[...], preferred_element_type=F32)
    mu = jnp.mean(r, axis=-1, keepdims=True)
    d = r - mu
    var = jnp.mean(d * d, axis=-1, keepdims=True)
    return d * lax.rsqrt(var + LN_EPS) * g_ref[...] + b_ref[...]


def _prompt_kernel(x_ref, xn_ref, mem_ref, w_q_ref, w_rest_ref, w_kvt_ref, w_memt_ref, w_out_ref,
                   u2_ref, cw_ref, cb_ref,
                   cg_ref, cbeta_ref, lg_ref, lb_ref,
                   y_ref, kt_ref, vt_ref, conv_ref, mkt_ref, mvt_ref,
                   h_scr, kt_scr, vt_scr, mkt_scr, mvt_scr, qm_scr, acc_scr, carry_scr,
                   upad_scr, mix_scr, wkvt_scr, *, tm, alpha):
    t = pl.program_id(1)
    n_sub = tm // TQ
    assert tm % CONV_ROWS == 0
    w_in_refs = (w_q_ref, w_rest_ref)
    row, col = _stacked_index(TQ)
    causal = col < row

    @pl.when(t == 0)
    def _():
        kvt = lax.dot_general(w_memt_ref[...], mem_ref[0].astype(BF16), NT_DIMS,
                              preferred_element_type=F32)
        mkt_ref[0] = kvt[:MEM_WIDTH]
        mvt_ref[0] = kvt[MEM_WIDTH:]
        mkt_scr[...] = kvt[:MEM_WIDTH].astype(BF16)
        mvt_scr[...] = kvt[MEM_WIDTH:].astype(BF16)
        upad_scr[0:HALO, :] = jnp.zeros((HALO, CONV_WIDTH), F32)

    h_cur = h_scr.at[t % 2]
    h_next = h_scr.at[(t + 1) % 2]
    xb = x_ref[0].astype(BF16)

    @pl.when(t == 0)
    def _():
        for c in range(N_CHUNKS):
            _project(xb, w_in_refs, h_scr.at[0], c)

    @pl.when(jnp.logical_and(pl.program_id(0) == 0, t == 0))
    def _():
        wkvt_scr[...] = w_kvt_ref[...].T

    kvt = lax.dot_general(wkvt_scr[...], xb, NT_DIMS, preferred_element_type=F32)
    kt_ref[0] = kvt[:SB_WIDTH]
    vt_ref[0] = kvt[SB_WIDTH:]
    blk0 = t * n_sub
    for c in range(n_sub):
        kt_scr[blk0 + c] = kvt[:SB_WIDTH, c * TK:(c + 1) * TK].astype(BF16)
        vt_scr[blk0 + c] = kvt[SB_WIDTH:, c * TK:(c + 1) * TK].astype(BF16)

    for i in range(n_sub):
        _build_qm(h_cur, slice(i * TQ, (i + 1) * TQ), qm_scr.at[i])
    acc_scr[...] = jnp.zeros(acc_scr.shape, F32)
    carry_scr[...] = jnp.zeros(carry_scr.shape, F32)
    carries = [carry_scr.at[i] for i in range(n_sub)]

    def refs(i, blk):
        return (qm_scr.at[i], acc_scr.at[i], carry_scr.at[i],
                lambda p: kt_scr[blk, p * LANES:(p + 1) * LANES, :],
                lambda p: vt_scr[blk, p * LANES:(p + 1) * LANES, :])

    def whole(j, subs=range(n_sub)):
        return [_whole_part(*refs(i, blk0 + i - 1 - j), None) for i in subs]

    def some_rows(j, lo, hi):
        return [_rows_part(*refs(i, blk0 + i - 1 - j), lo, hi) for i in range(n_sub)]

    def left():
        return _carry_max(carries, 0, TOP_ROWS), _carry_max(carries, TOP_ROWS, TQ)

    xb_next = xn_ref[0].astype(BF16)
    for c in range(PROJ_SPLIT):
        _project(xb_next, w_in_refs, h_next, c)
    _sb_blocks([_whole_part(*refs(i, blk0 + i), causal) for i in range(n_sub)], u2_ref)

    def speculate():
        parts = [part for both in zip(whole(0), some_rows(1, 0, TOP_ROWS)) for part in both]
        _sb_blocks(parts, u2_ref)
        return left()

    def all_left():
        return jnp.float32(0.0), jnp.float32(0.0)

    ahead = blk0 >= 2
    m_top, m_bot = lax.cond(ahead, speculate, all_left)

    def catch_up():
        _sb_blocks(some_rows(1, TOP_ROWS, TQ), u2_ref)
        return _carry_max(carries, TOP_ROWS, TQ)

    m_bot = lax.cond(jnp.logical_and(ahead, m_bot > SKIP_LOG2), catch_up, lambda: m_bot)

    def trip_cond(c):
        j, top, bot = c
        return jnp.logical_and(j < blk0, jnp.maximum(top, bot) > SKIP_LOG2)

    def trip(c):
        j, _, bot = c

        def every_row():
            _sb_blocks(whole(j), u2_ref)
            return left()

        def first_rows():
            _sb_blocks(some_rows(j, 0, TOP_ROWS), u2_ref)
            return _carry_max(carries, 0, TOP_ROWS), bot

        return (j + 1,) + lax.cond(bot > SKIP_LOG2, every_row, first_rows)

    first_trip = jnp.where(ahead, 2, 0).astype(jnp.int32)
    j, m_top, m_bot = lax.while_loop(trip_cond, trip, (first_trip, m_top, m_bot))
    for d in range(n_sub - 1):
        def tail(d=d):
            _sb_blocks(whole(blk0 + d, range(d + 1, n_sub)), u2_ref)
            return left()
        m_top, m_bot = lax.cond(
            jnp.logical_and(j >= blk0, jnp.maximum(m_top, m_bot) > SKIP_LOG2),
            tail, lambda: (m_top, m_bot))
    for c in range(PROJ_SPLIT, N_CHUNKS):
        _project(xb_next, w_in_refs, h_next, c)
    for i in range(n_sub):
        _sb_finish(h_cur, slice(i * TQ, (i + 1) * TQ), acc_scr.at[i], mix_scr)

    u = (_hcols(h_cur, slice(0, tm), COL_CA, CONV_WIDTH)
         * _sigmoid(_hcols(h_cur, slice(0, tm), COL_CB, CONV_WIDTH)))
    upad_scr[HALO:HALO + tm, :] = u

    for r in range(0, tm, CONV_ROWS):
        _conv_rows(h_cur, r, upad_scr, r, CONV_ROWS, mix_scr, cw_ref, cb_ref, cg_ref, cbeta_ref)
    last_rows = upad_scr[tm:tm + HALO, :]
    upad_scr[0:HALO, :] = last_rows
    conv_ref[0] = last_rows

    _mem_attend(h_cur, slice(0, tm), tm, mkt_scr[...], mvt_scr[...], mix_scr)
    y_ref[0] = _out_norm(x_ref[0], mix_scr, w_out_ref, lg_ref, lb_ref, alpha)


def _const_spec(shape):
    return pl.BlockSpec(shape, lambda *_: (0,) * len(shape), pipeline_mode=pl.Buffered(1))


def _prompt_call(x, mem, w_q, w_rest, w_kvt, w_memt, w_out, u2, cw, cb, cg, cbeta, lg, lb, alpha, tm):
    B, T, _ = x.shape
    assert T % tm == 0 and tm % TQ == 0 and TQ == TK
    grid = (B, T // tm)
    row_spec = lambda width: pl.BlockSpec((1, tm, width), lambda b, t: (b, t, 0))
    col_spec = lambda height: pl.BlockSpec((1, height, tm), lambda b, t: (b, 0, t))
    per_b = lambda r, width: pl.BlockSpec((1, r, width), lambda b, t: (b, 0, 0))
    out_shape = (
        jax.ShapeDtypeStruct((B, T, D_MODEL), F32),
        jax.ShapeDtypeStruct((B, SB_WIDTH, T), F32),
        jax.ShapeDtypeStruct((B, SB_WIDTH, T), F32),
        jax.ShapeDtypeStruct((B, HALO, CONV_WIDTH), F32),
        jax.ShapeDtypeStruct((B, MEM_WIDTH, N_MEM), F32),
        jax.ShapeDtypeStruct((B, MEM_WIDTH, N_MEM), F32),
    )
    scratch = [
        pltpu.VMEM((2, N_CHUNKS, tm, PROJ_COLS), F32),
        pltpu.VMEM((T // TK, SB_WIDTH, TK), BF16),
        pltpu.VMEM((T // TK, SB_WIDTH, TK), BF16),
        pltpu.VMEM((MEM_WIDTH, N_MEM), BF16),
        pltpu.VMEM((MEM_WIDTH, N_MEM), BF16),
        pltpu.VMEM((tm // TQ, PAIRS, 2 * TQ, LANES), BF16),
        pltpu.VMEM((tm // TQ, PAIRS, 2 * TQ, LANES), F32),
        pltpu.VMEM((tm // TQ, PAIRS, 2 * TQ, LANES), F32),
        pltpu.VMEM((HALO + tm, CONV_WIDTH), F32),
        pltpu.VMEM((tm, D_MODEL), BF16),
        pltpu.VMEM((2 * SB_WIDTH, D_MODEL), BF16),
    ]
    return pl.pallas_call(
        functools.partial(_prompt_kernel, tm=tm, alpha=alpha),
        grid=grid,
        in_specs=[
            row_spec(D_MODEL),
            pl.BlockSpec((1, tm, D_MODEL), lambda b, t: (b, jnp.minimum(t + 1, T // tm - 1), 0)),
            per_b(N_MEM, D_MODEL),
            _const_spec((D_MODEL, SB_WIDTH)),
            _const_spec((D_MODEL, H_WIDTH - SB_WIDTH)),
            _const_spec((2 * SB_WIDTH, D_MODEL)),
            _const_spec((2 * MEM_WIDTH, D_MODEL)),
            _const_spec((D_MODEL, D_MODEL)),
            _const_spec((TK, 2 * TK)),
            _const_spec((CONV_K, CONV_WIDTH)),
            _const_spec((1, CONV_WIDTH)),
            _const_spec((1, CONV_WIDTH)),
            _const_spec((1, CONV_WIDTH)),
            _const_spec((1, D_MODEL)),
            _const_spec((1, D_MODEL)),
        ],
        out_specs=(
            row_spec(D_MODEL), col_spec(SB_WIDTH), col_spec(SB_WIDTH),
            per_b(HALO, CONV_WIDTH), per_b(MEM_WIDTH, N_MEM), per_b(MEM_WIDTH, N_MEM),
        ),
        out_shape=out_shape,
        scratch_shapes=scratch,
        compiler_params=pltpu.CompilerParams(
            dimension_semantics=("arbitrary", "arbitrary"),
            vmem_limit_bytes=VMEM_LIMIT),
        name="prompt_layer",
    )(x, x, mem, w_q, w_rest, w_kvt, w_memt, w_out, u2, cw, cb, cg, cbeta, lg, lb)


def _sample_kernel(x_ref, ktwin_ref, vtwin_ref, ktc_hbm, vtc_hbm, cpast_ref, mktc_ref, mvtc_ref,
                   w_q_ref, w_rest_ref, w_kv_ref, w_kvt_ref, w_out_ref, u2_ref, cw_ref, cb_ref, cg_ref,
                   cbeta_ref,
                   lg_ref, lb_ref,
                   y_ref, k_ref, v_ref, conv_ref,
                   h_scr, ktnew_scr, vtnew_scr, ktbuf, vtbuf, sem, qm_scr, acc_scr, carry_scr,
                   upad_scr, mix_scr, *, group, n_new, win, past, alpha):
    step = pl.program_id(0)
    row, col = _stacked_index(n_new)
    n_win = win // TK
    n_blocks = past // TK

    xb = x_ref[...].astype(BF16)
    for c in range(IN_WIDTH // PROJ_COLS):
        _project(xb, (w_q_ref, w_rest_ref, w_kv_ref), h_scr, c)
    for c in range(SB_WIDTH // PROJ_COLS):
        cols = slice(c * PROJ_COLS, (c + 1) * PROJ_COLS)
        k_ref[:, cols] = h_scr[COL_K // PROJ_COLS + c]
        v_ref[:, cols] = h_scr[COL_V // PROJ_COLS + c]
    kvt = lax.dot_general(w_kvt_ref[...], xb, TN_NT_DIMS, preferred_element_type=F32)
    ktnew_scr[...] = kvt[:SB_WIDTH].astype(BF16)
    vtnew_scr[...] = kvt[SB_WIDTH:].astype(BF16)

    for g in range(group):
        _build_qm(h_scr, slice(g * n_new, (g + 1) * n_new), qm_scr.at[g])
    acc_scr[...] = jnp.zeros(acc_scr.shape, F32)
    carry_scr[...] = jnp.zeros(carry_scr.shape, F32)
    carries = [carry_scr.at[g] for g in range(group)]

    def part(g, get_kt, get_vt, mask):
        return _whole_part(qm_scr.at[g], acc_scr.at[g], carry_scr.at[g], get_kt, get_vt, mask)

    def own_past(g):
        rel = col - g * n_new
        return jnp.logical_and(rel >= 0, rel < row)

    _sb_blocks([part(g, lambda p: ktnew_scr[p * LANES:(p + 1) * LANES, :],
                     lambda p: vtnew_scr[p * LANES:(p + 1) * LANES, :], own_past(g))
                for g in range(group)], u2_ref)

    def win_blocks(j):
        cols = slice(win - (j + 1) * TK, win - j * TK)
        _sb_blocks([part(g, lambda p, g=g: ktwin_ref[g, p * LANES:(p + 1) * LANES, cols].astype(BF16),
                         lambda p, g=g: vtwin_ref[g, p * LANES:(p + 1) * LANES, cols].astype(BF16),
                         None)
                    for g in range(group)], u2_ref)
        return _carry_max(carries)

    m = jnp.float32(0.0)
    for j in range(n_win):
        m = lax.cond(m > SKIP_LOG2, functools.partial(win_blocks, j), lambda m=m: m)

    def far_walk(g, _):
        stream = step * group + g

        def far_block(j):
            start = pl.multiple_of(past - (j + 1) * TK, TK)
            ck = pltpu.make_async_copy(ktc_hbm.at[stream, :, pl.ds(start, TK)], ktbuf, sem.at[0])
            cv = pltpu.make_async_copy(vtc_hbm.at[stream, :, pl.ds(start, TK)], vtbuf, sem.at[1])
            ck.start()
            cv.start()
            ck.wait()
            cv.wait()
            _sb_blocks([part(g, lambda p: ktbuf[p * LANES:(p + 1) * LANES, :].astype(BF16),
                             lambda p: vtbuf[p * LANES:(p + 1) * LANES, :].astype(BF16), None)],
                       u2_ref)

        own = [carry_scr.at[g]]
        _sb_walk(jnp.int32(n_win), n_blocks, _carry_max(own), far_block, own)
        return 0

    lax.fori_loop(0, group, far_walk, 0)
    for g in range(group):
        rows = slice(g * n_new, (g + 1) * n_new)
        _sb_finish(h_scr, rows, acc_scr.at[g], mix_scr)
        _mem_attend(h_scr, rows, n_new, mktc_ref[g].astype(BF16), mvtc_ref[g].astype(BF16), mix_scr)

    u = (_hcols(h_scr, slice(0, group * n_new), COL_CA, CONV_WIDTH)
         * _sigmoid(_hcols(h_scr, slice(0, group * n_new), COL_CB, CONV_WIDTH)))
    for g in range(group):
        base = g * (HALO + n_new)
        upad_scr[base:base + HALO, :] = cpast_ref[g]
        upad_scr[base + HALO:base + HALO + n_new, :] = u[g * n_new:(g + 1) * n_new]
        _conv_rows(h_scr, g * n_new, upad_scr, base, n_new, mix_scr, cw_ref, cb_ref, cg_ref, cbeta_ref)
        conv_ref[g] = upad_scr[base + n_new:base + n_new + HALO, :]

    y_ref[...] = _out_norm(x_ref[...], mix_scr, w_out_ref, lg_ref, lb_ref, alpha)


def _sample_call(x, ktc, vtc, cpast, mktc, mvtc, w_q, w_rest, w_kv, w_kvt, w_out, u2, cw, cb, cg, cbeta,
                 lg, lb,
                 alpha, win):
    S, n_new, _ = x.shape
    past = ktc.shape[2]
    group = TK // n_new
    assert group * n_new == TK and S % group == 0 and past % win == 0 and win % TK == 0
    assert n_new % 16 == 0 and n_new >= CONV_STATE
    rows = group * n_new
    x2 = x.reshape(S * n_new, D_MODEL)
    row_spec = lambda width: pl.BlockSpec((rows, width), lambda i: (i, 0))
    grp = lambda r, width: pl.BlockSpec((group, r, width), lambda i: (i, 0, 0))
    win_spec = pl.BlockSpec((group, SB_WIDTH, win), lambda i: (i, 0, past // win - 1))
    any_spec = pl.BlockSpec(memory_space=pl.ANY)
    out_shape = (
        jax.ShapeDtypeStruct((S * n_new, D_MODEL), F32),
        jax.ShapeDtypeStruct((S * n_new, SB_WIDTH), F32),
        jax.ShapeDtypeStruct((S * n_new, SB_WIDTH), F32),
        jax.ShapeDtypeStruct((S, HALO, CONV_WIDTH), F32),
    )
    scratch = [
        pltpu.VMEM((IN_WIDTH // PROJ_COLS, rows, PROJ_COLS), F32),
        pltpu.VMEM((SB_WIDTH, TK), BF16),
        pltpu.VMEM((SB_WIDTH, TK), BF16),
        pltpu.VMEM((SB_WIDTH, TK), F32),
        pltpu.VMEM((SB_WIDTH, TK), F32),
        pltpu.SemaphoreType.DMA((2,)),
        pltpu.VMEM((group, PAIRS, 2 * n_new, LANES), BF16),
        pltpu.VMEM((group, PAIRS, 2 * n_new, LANES), F32),
        pltpu.VMEM((group, PAIRS, 2 * n_new, LANES), F32),
        pltpu.VMEM((group * (HALO + n_new), CONV_WIDTH), F32),
        pltpu.VMEM((rows, D_MODEL), BF16),
    ]
    return pl.pallas_call(
        functools.partial(_sample_kernel, group=group, n_new=n_new, win=win, past=past, alpha=alpha),
        grid=(S // group,),
        in_specs=[
            row_spec(D_MODEL), win_spec, win_spec, any_spec, any_spec,
            grp(HALO, CONV_WIDTH), grp(MEM_WIDTH, N_MEM), grp(MEM_WIDTH, N_MEM),
            _const_spec((D_MODEL, SB_WIDTH)),
            _const_spec((D_MODEL, H_WIDTH - SB_WIDTH)),
            _const_spec((D_MODEL, 2 * SB_WIDTH)),
            _const_spec((2 * SB_WIDTH, D_MODEL)),
            _const_spec((D_MODEL, D_MODEL)),
            _const_spec((TK, 2 * TK)),
            _const_spec((CONV_K, CONV_WIDTH)),
            _const_spec((1, CONV_WIDTH)),
            _const_spec((1, CONV_WIDTH)),
            _const_spec((1, CONV_WIDTH)),
            _const_spec((1, D_MODEL)),
            _const_spec((1, D_MODEL)),
        ],
        out_specs=(row_spec(D_MODEL), row_spec(SB_WIDTH), row_spec(SB_WIDTH), grp(HALO, CONV_WIDTH)),
        out_shape=out_shape,
        scratch_shapes=scratch,
        compiler_params=pltpu.CompilerParams(
            dimension_semantics=("arbitrary",),
            vmem_limit_bytes=VMEM_LIMIT),
        name="sample_layer",
    )(x2, ktc, vtc, ktc, vtc, cpast, mktc, mvtc, w_q, w_rest, w_kv, w_kvt, w_out, u2, cw, cb, cg,
      cbeta, lg, lb)


def _prefix_matrix():
    j = lax.broadcasted_iota(jnp.int32, (TK, TK), 0)
    s = lax.broadcasted_iota(jnp.int32, (TK, TK), 1)
    return jnp.concatenate([(j > s).astype(BF16), jnp.ones((TK, TK), BF16)], axis=1)


def _time_minor(a):
    n, time, heads, dim = a.shape
    return jnp.transpose(a, (0, 2, 3, 1)).reshape(n, heads * dim, time)


def _time_major(a, heads):
    n, width, time = a.shape
    return jnp.transpose(a.reshape(n, heads, width // heads, time), (0, 3, 1, 2))[None]


def kernel(x_prompt, x_sample, cache_sb_k, cache_sb_v, cache_conv, cache_mem_k, cache_mem_v, mem_prompt, w_in, w_mem_kv, conv_w, conv_b, conv_ln_g, conv_ln_b, w_out, ln_g, ln_b):
    depth = w_in.shape[0]
    assert depth == 1, "single layer only"
    alpha = (2 * depth) ** 0.25
    S, n_new, _ = x_sample.shape

    u2 = _prefix_matrix()
    w = w_in[0]
    w_q = w[:, :REF_K].astype(BF16)
    w_rest = w[:, REF_V_END:].astype(BF16)
    w_kv = w[:, REF_K:REF_V_END].astype(BF16)
    w_kvt = w_kv
    w_memt = w_mem_kv[0].T.astype(BF16)
    w_out_b = w_out[0].astype(BF16)
    cw = conv_w[0]
    cb = conv_b[0].reshape(1, CONV_WIDTH)
    cg = conv_ln_g[0].reshape(1, CONV_WIDTH)
    cbeta = conv_ln_b[0].reshape(1, CONV_WIDTH)
    lg = ln_g[0].reshape(1, D_MODEL)
    lb = ln_b[0].reshape(1, D_MODEL)

    yp, ktp, vtp, cp, mktp, mvtp = _prompt_call(
        x_prompt, mem_prompt, w_q, w_rest, w_kvt, w_memt, w_out_b, u2, cw, cb, cg, cbeta, lg, lb,
        alpha, tm=PROMPT_TILE)

    cpast = jnp.pad(cache_conv[0], ((0, 0), (HALO_PAD, 0), (0, 0)))
    ys, ks, vs, cs = _sample_call(
        x_sample, _time_minor(cache_sb_k[0]), _time_minor(cache_sb_v[0]), cpast,
        _time_minor(cache_mem_k[0]), _time_minor(cache_mem_v[0]),
        w_q, w_rest, w_kv, w_kvt, w_out_b, u2, cw, cb, cg, cbeta, lg, lb, alpha, win=CACHE_WINDOW)

    return (
        yp,
        ys.reshape(S, n_new, D_MODEL),
        _time_major(ktp, SB_HEADS),
        _time_major(vtp, SB_HEADS),
        cp[None, :, HALO_PAD:, :],
        _time_major(mktp, MEM_HEADS),
        _time_major(mvtp, MEM_HEADS),
        ks.reshape(1, S, n_new, SB_HEADS, HEAD_DIM),
        vs.reshape(1, S, n_new, SB_HEADS, HEAD_DIM),
        cs[None, :, HALO_PAD:, :],
    )
```

```python
import functools
import math
from typing import Any, Callable, NamedTuple

import jax
import jax.numpy as jnp
from jax import lax
from jax.experimental import pallas as pl
from jax.experimental.pallas import tpu as pltpu

F32 = jnp.float32
BF16 = jnp.bfloat16

D_MODEL = 1024
SB_HEADS = 8
HEAD_DIM = 64
SB_WIDTH = SB_HEADS * HEAD_DIM
CONV_WIDTH = 256
CONV_K = 31
CONV_STATE = CONV_K - 1
MEM_HEADS = 4
MEM_WIDTH = MEM_HEADS * HEAD_DIM
N_MEM = 256
IN_WIDTH = 4 * SB_WIDTH + 3 * CONV_WIDTH + 2 * MEM_WIDTH
REF_K, REF_V_END = SB_WIDTH, 3 * SB_WIDTH
COL_Q = 0
COL_GA = SB_WIDTH
COL_CA = COL_GA + SB_WIDTH
COL_CB = COL_CA + CONV_WIDTH
COL_GC = COL_CB + CONV_WIDTH
COL_QM = COL_GC + CONV_WIDTH
COL_GM = COL_QM + MEM_WIDTH
H_WIDTH = COL_GM + MEM_WIDTH
COL_K = H_WIDTH
COL_V = COL_K + SB_WIDTH
COL_MIX_A, COL_MIX_C, COL_MIX_M = 0, SB_WIDTH, SB_WIDTH + CONV_WIDTH
LN_EPS = 1e-5
QK_SCALE = HEAD_DIM ** -0.5

LANES = 128
SUBLANES = 8
PAIRS = SB_HEADS // 2
TQ = 128
TK = 128
HALO = 32
HALO_PAD = HALO - CONV_STATE
CONV_ROWS = 256
PROJ_COLS = 256
SB_WAVE = 2
SB_SKEW = 2
TOP_ROWS = 32
N_CHUNKS = H_WIDTH // PROJ_COLS
PROJ_SPLIT = 4
PROMPT_TILE = 2 * TQ
CACHE_WINDOW = 2 * TK
INV_LN2 = 1.0 / math.log(2.0)
QK_SCALE_LOG2 = QK_SCALE * INV_LN2
SKIP_LOG2 = -152.0
VMEM_LIMIT = 60 * 1024 * 1024

NT_DIMS = (((1,), (1,)), ((), ()))
TN_NT_DIMS = (((0,), (1,)), ((), ()))


def _sigmoid(x):
    return 1.0 / (1.0 + jnp.exp(-x))


def _silu(x):
    return x * _sigmoid(x)


def _project(xb, w_refs, h_ref, chunk):
    col = chunk * PROJ_COLS
    for w_ref in w_refs:
        if col < w_ref.shape[1]:
            break
        col -= w_ref.shape[1]
    h_ref[chunk] = jnp.dot(xb, w_ref[:, col:col + PROJ_COLS], preferred_element_type=F32)


def _hcols(h_ref, rows, col, width):
    off = col % PROJ_COLS
    assert off + width <= PROJ_COLS
    return h_ref[col // PROJ_COLS, rows, off:off + width]


def _lane_lo(n_rows):
    return lax.broadcasted_iota(jnp.int32, (n_rows, LANES), 1) < HEAD_DIM


def _build_qm(h_ref, rows, qm_ref):
    n = qm_ref.shape[1] // 2
    lane_lo = _lane_lo(n)
    for p in range(PAIRS):
        qp = _hcols(h_ref, rows, COL_Q + p * LANES, LANES) * QK_SCALE_LOG2
        qm_ref[p, 0:n] = jnp.where(lane_lo, qp, 0.0).astype(BF16)
        qm_ref[p, n:2 * n] = jnp.where(lane_lo, 0.0, qp).astype(BF16)


class _Part(NamedTuple):
    qm: Callable[[int], Any]
    carry_get: Callable[[int], Any]
    carry_set: Callable[[int, Any], None]
    acc_add: Callable[[int, Any], None]
    get_kt: Callable[[int], Any]
    get_vt: Callable[[int], Any]
    mask: Any


def _whole_part(qm_ref, acc_ref, carry_ref, get_kt, get_vt, mask):
    def carry_set(p, v):
        carry_ref[p] = v

    def acc_add(p, v):
        acc_ref[p] += v

    return _Part(lambda p: qm_ref[p], lambda p: carry_ref[p], carry_set, acc_add,
                 get_kt, get_vt, mask)


def _rows_part(qm_ref, acc_ref, carry_ref, get_kt, get_vt, lo, hi):
    n = qm_ref.shape[1] // 2
    k = hi - lo
    head_a, head_b = slice(lo, hi), slice(n + lo, n + hi)

    def gather(ref, p):
        return jnp.concatenate([ref[p, head_a], ref[p, head_b]], axis=0)

    def carry_set(p, v):
        carry_ref[p, head_a] = v[:k]
        carry_ref[p, head_b] = v[k:]

    def acc_add(p, v):
        acc_ref[p, head_a] += v[:k]
        acc_ref[p, head_b] += v[k:]

    return _Part(lambda p: gather(qm_ref, p), lambda p: gather(carry_ref, p), carry_set, acc_add,
                 get_kt, get_vt, None)


def _sb_blocks(parts, u2_ref):
    chains = [(part, p) for part in parts for p in range(PAIRS)]
    waves = [chains[i:i + SB_WAVE] for i in range(0, len(chains), SB_WAVE)]

    def scores(wave):
        return [jnp.dot(part.qm(p), part.get_kt(p), preferred_element_type=F32)
                for part, p in wave]

    def prefix_sums(wave, zs):
        hits, fails = [], []
        for (part, _), z in zip(wave, zs):
            nz = -z
            soft = jnp.log(1.0 + jnp.exp2(jnp.minimum(z, nz))) * INV_LN2
            log_fail = jnp.minimum(nz, 0.0) - soft
            hits.append(log_fail + z)
            if part.mask is not None:
                log_fail = jnp.where(part.mask, log_fail, 0.0)
            fails.append(log_fail.astype(BF16))
        sums = jnp.dot(jnp.concatenate(fails, axis=0), u2_ref[...],
                       preferred_element_type=F32)
        return hits, sums

    def accumulate(wave, hits, sums):
        row = 0
        for (part, p), log_hit in zip(wave, hits):
            s = sums[row:row + log_hit.shape[0]]
            row += log_hit.shape[0]
            carry = part.carry_get(p)
            w = jnp.exp2(log_hit + s[:, :TK] + carry)
            if part.mask is not None:
                w = jnp.where(part.mask, w, 0.0)
            part.carry_set(p, carry + s[:, TK:])
            part.acc_add(p, lax.dot_general(w.astype(BF16), part.get_vt(p), NT_DIMS,
                                            preferred_element_type=F32))

    zs, summed = {}, {}
    for k in range(len(waves) + 2 * SB_SKEW):
        if k < len(waves):
            zs[k] = scores(waves[k])
        if 0 <= k - SB_SKEW < len(waves):
            summed[k - SB_SKEW] = prefix_sums(waves[k - SB_SKEW], zs.pop(k - SB_SKEW))
        if 0 <= k - 2 * SB_SKEW < len(waves):
            accumulate(waves[k - 2 * SB_SKEW], *summed.pop(k - 2 * SB_SKEW))


def _carry_max(carry_refs, lo=0, hi=None):
    n = carry_refs[0].shape[1] // 2
    hi = n if hi is None else hi
    m = None
    for ref in carry_refs:
        for p in range(PAIRS):
            for rows in (slice(lo, hi), slice(n + lo, n + hi)):
                m = ref[p, rows] if m is None else jnp.maximum(m, ref[p, rows])
    return jnp.max(m)


def _sb_walk(first, n_blocks, m0, block_fn, carry_refs):
    def cond(c):
        j, m = c
        return jnp.logical_and(j < n_blocks, m > SKIP_LOG2)

    def body(c):
        j, _ = c
        block_fn(j)
        return j + 1, _carry_max(carry_refs)

    return lax.while_loop(cond, body, (first, m0))


def _sb_finish(h_ref, rows, acc_ref, mix_ref):
    n = acc_ref.shape[1] // 2
    lane_lo = _lane_lo(n)
    for p in range(PAIRS):
        o = jnp.where(lane_lo, acc_ref[p, 0:n], acc_ref[p, n:2 * n])
        g = _hcols(h_ref, rows, COL_GA + p * LANES, LANES)
        mix_ref[rows, COL_MIX_A + p * LANES:COL_MIX_A + (p + 1) * LANES] = (
            o * _silu(g)).astype(BF16)


def _stacked_index(n_rows):
    row = lax.broadcasted_iota(jnp.int32, (2 * n_rows, TK), 0)
    col = lax.broadcasted_iota(jnp.int32, (2 * n_rows, TK), 1)
    return jnp.where(row >= n_rows, row - n_rows, row), col


def _conv_rows(h_ref, row, upad_ref, urow, nr, mix_ref, cw_ref, cb_ref, g_ref, b_ref):
    c = jnp.broadcast_to(cb_ref[...], (nr, CONV_WIDTH))
    for s in range(SUBLANES):
        taps = [i for i in range(CONV_K) if (HALO_PAD + i) % SUBLANES == s]
        span = nr if s == 0 else nr + SUBLANES
        part = None
        for i in taps:
            term = cw_ref[i:i + 1, :] * upad_ref[pl.ds(urow + (HALO_PAD + i - s), span), :]
            part = term if part is None else part + term
        c = c + part[s:s + nr]
    mu = jnp.mean(c, axis=-1, keepdims=True)
    d = c - mu
    var = jnp.mean(d * d, axis=-1, keepdims=True)
    n = d * lax.rsqrt(var + LN_EPS) * g_ref[...] + b_ref[...]
    gate = _hcols(h_ref, pl.ds(row, nr), COL_GC, CONV_WIDTH)
    mix_ref[pl.ds(row, nr), COL_MIX_C:COL_MIX_C + CONV_WIDTH] = (_silu(n) * _silu(gate)).astype(BF16)


def _mem_attend(h_ref, rows, n_rows, mkt, mvt, mix_ref):
    lane_lo = _lane_lo(n_rows)
    for p in range(MEM_HEADS // 2):
        qp = _hcols(h_ref, rows, COL_QM + p * LANES, LANES) * QK_SCALE
        mkp = mkt[p * LANES:(p + 1) * LANES, :]
        mvp = mvt[p * LANES:(p + 1) * LANES, :]
        qh = jnp.concatenate([jnp.where(lane_lo, qp, 0.0), jnp.where(lane_lo, 0.0, qp)],
                             axis=0).astype(BF16)
        s = jnp.dot(qh, mkp, preferred_element_type=F32)
        e = jnp.exp(s - jnp.max(s, axis=-1, keepdims=True))
        o = lax.dot_general(e.astype(BF16), mvp, NT_DIMS, preferred_element_type=F32)
        o = o / jnp.sum(e, axis=-1, keepdims=True)
        o = jnp.where(lane_lo, o[:n_rows], o[n_rows:])
        g = _hcols(h_ref, rows, COL_GM + p * LANES, LANES)
        mix_ref[rows, COL_MIX_M + p * LANES:COL_MIX_M + (p + 1) * LANES] = (o * _silu(g)).astype(BF16)


def _out_norm(x, mix_ref, w_out_ref, g_ref, b_ref, alpha):
    r = alpha * x + jnp.dot(mix_ref[...], w_out_ref[...], preferred_element_type=F32)
    mu = jnp.mean(r, axis=-1, keepdims=True)
    d = r - mu
    var = jnp.mean(d * d, axis=-1, keepdims=True)
    return d * lax.rsqrt(var + LN_EPS) * g_ref[...] + b_ref[...]


def _prompt_kernel(x_ref, xn_ref, mem_ref, w_q_ref, w_rest_ref, w_kv_ref, w_memt_ref, w_out_ref,
                   u2_ref, cw_ref, cb_ref,
                   cg_ref, cbeta_ref, lg_ref, lb_ref,
                   y_ref, kt_ref, vt_ref, conv_ref, mkt_ref, mvt_ref,
                   h_scr, kt_scr, vt_scr, mkt_scr, mvt_scr, qm_scr, acc_scr, carry_scr,
                   upad_scr, mix_scr, wkvt_scr, *, tm, alpha):
    t = pl.program_id(1)
    n_sub = tm // TQ
    assert tm % CONV_ROWS == 0
    w_in_refs = (w_q_ref, w_rest_ref)
    row, col = _stacked_index(TQ)
    causal = col < row

    @pl.when(t == 0)
    def _():
        kvt = lax.dot_general(w_memt_ref[...], mem_ref[0].astype(BF16), NT_DIMS,
                              preferred_element_type=F32)
        mkt_ref[0] = kvt[:MEM_WIDTH]
        mvt_ref[0] = kvt[MEM_WIDTH:]
        mkt_scr[...] = kvt[:MEM_WIDTH].astype(BF16)
        mvt_scr[...] = kvt[MEM_WIDTH:].astype(BF16)
        upad_scr[0:HALO, :] = jnp.zeros((HALO, CONV_WIDTH), F32)

    h_cur = h_scr.at[t % 2]
    h_next = h_scr.at[(t + 1) % 2]
    xb = x_ref[0].astype(BF16)

    @pl.when(t == 0)
    def _():
        for c in range(N_CHUNKS):
            _project(xb, w_in_refs, h_scr.at[0], c)

    @pl.when(jnp.logical_and(pl.program_id(0) == 0, t == 0))
    def _():
        wkvt_scr[...] = w_kv_ref[...].T

    kvt = lax.dot_general(wkvt_scr[...], xb, NT_DIMS, preferred_element_type=F32)
    kt_ref[0] = kvt[:SB_WIDTH]
    vt_ref[0] = kvt[SB_WIDTH:]
    blk0 = t * n_sub
    for c in range(n_sub):
        kt_scr[blk0 + c] = kvt[:SB_WIDTH, c * TK:(c + 1) * TK].astype(BF16)
        vt_scr[blk0 + c] = kvt[SB_WIDTH:, c * TK:(c + 1) * TK].astype(BF16)

    for i in range(n_sub):
        _build_qm(h_cur, slice(i * TQ, (i + 1) * TQ), qm_scr.at[i])
    acc_scr[...] = jnp.zeros(acc_scr.shape, F32)
    carry_scr[...] = jnp.zeros(carry_scr.shape, F32)
    carries = [carry_scr.at[i] for i in range(n_sub)]

    def refs(i, blk):
        return (qm_scr.at[i], acc_scr.at[i], carry_scr.at[i],
                lambda p: kt_scr[blk, p * LANES:(p + 1) * LANES, :],
                lambda p: vt_scr[blk, p * LANES:(p + 1) * LANES, :])

    def whole(j, subs=range(n_sub)):
        return [_whole_part(*refs(i, blk0 + i - 1 - j), None) for i in subs]

    def some_rows(j, lo, hi):
        return [_rows_part(*refs(i, blk0 + i - 1 - j), lo, hi) for i in range(n_sub)]

    def left():
        return _carry_max(carries, 0, TOP_ROWS), _carry_max(carries, TOP_ROWS, TQ)

    xb_next = xn_ref[0].astype(BF16)
    for c in range(PROJ_SPLIT):
        _project(xb_next, w_in_refs, h_next, c)
    _sb_blocks([_whole_part(*refs(i, blk0 + i), causal) for i in range(n_sub)], u2_ref)

    def speculate():
        parts = [part for both in zip(whole(0), some_rows(1, 0, TOP_ROWS)) for part in both]
        _sb_blocks(parts, u2_ref)
        return left()

    def all_left():
        return jnp.float32(0.0), jnp.float32(0.0)

    ahead = blk0 >= 2
    m_top, m_bot = lax.cond(ahead, speculate, all_left)

    def catch_up():
        _sb_blocks(some_rows(1, TOP_ROWS, TQ), u2_ref)
        return _carry_max(carries, TOP_ROWS, TQ)

    m_bot = lax.cond(jnp.logical_and(ahead, m_bot > SKIP_LOG2), catch_up, lambda: m_bot)

    def trip_cond(c):
        j, top, bot = c
        return jnp.logical_and(j < blk0, jnp.maximum(top, bot) > SKIP_LOG2)

    def trip(c):
        j, _, bot = c

        def every_row():
            _sb_blocks(whole(j), u2_ref)
            return left()

        def first_rows():
            _sb_blocks(some_rows(j, 0, TOP_ROWS), u2_ref)
            return _carry_max(carries, 0, TOP_ROWS), bot

        return (j + 1,) + lax.cond(bot > SKIP_LOG2, every_row, first_rows)

    first_trip = jnp.where(ahead, 2, 0).astype(jnp.int32)
    j, m_top, m_bot = lax.while_loop(trip_cond, trip, (first_trip, m_top, m_bot))
    for d in range(n_sub - 1):
        def tail(d=d):
            _sb_blocks(whole(blk0 + d, range(d + 1, n_sub)), u2_ref)
            return left()
        m_top, m_bot = lax.cond(
            jnp.logical_and(j >= blk0, jnp.maximum(m_top, m_bot) > SKIP_LOG2),
            tail, lambda: (m_top, m_bot))
    for c in range(PROJ_SPLIT, N_CHUNKS):
        _project(xb_next, w_in_refs, h_next, c)
    for i in range(n_sub):
        _sb_finish(h_cur, slice(i * TQ, (i + 1) * TQ), acc_scr.at[i], mix_scr)

    u = (_hcols(h_cur, slice(0, tm), COL_CA, CONV_WIDTH)
         * _sigmoid(_hcols(h_cur, slice(0, tm), COL_CB, CONV_WIDTH)))
    upad_scr[HALO:HALO + tm, :] = u

    for r in range(0, tm, CONV_ROWS):
        _conv_rows(h_cur, r, upad_scr, r, CONV_ROWS, mix_scr, cw_ref, cb_ref, cg_ref, cbeta_ref)
    last_rows = upad_scr[tm:tm + HALO, :]
    upad_scr[0:HALO, :] = last_rows
    conv_ref[0] = last_rows

    _mem_attend(h_cur, slice(0, tm), tm, mkt_scr[...], mvt_scr[...], mix_scr)
    y_ref[0] = _out_norm(x_ref[0], mix_scr, w_out_ref, lg_ref, lb_ref, alpha)


def _const_spec(shape):
    return pl.BlockSpec(shape, lambda *_: (0,) * len(shape), pipeline_mode=pl.Buffered(1))


def _prompt_call(x, mem, w_q, w_rest, w_kv, w_memt, w_out, u2, cw, cb, cg, cbeta, lg, lb, alpha, tm):
    B, T, _ = x.shape
    assert T % tm == 0 and tm % TQ == 0 and TQ == TK
    grid = (B, T // tm)
    row_spec = lambda width: pl.BlockSpec((1, tm, width), lambda b, t: (b, t, 0))
    col_spec = lambda height: pl.BlockSpec((1, height, tm), lambda b, t: (b, 0, t))
    per_b = lambda r, width: pl.BlockSpec((1, r, width), lambda b, t: (b, 0, 0))
    out_shape = (
        jax.ShapeDtypeStruct((B, T, D_MODEL), F32),
        jax.ShapeDtypeStruct((B, SB_WIDTH, T), F32),
        jax.ShapeDtypeStruct((B, SB_WIDTH, T), F32),
        jax.ShapeDtypeStruct((B, HALO, CONV_WIDTH), F32),
        jax.ShapeDtypeStruct((B, MEM_WIDTH, N_MEM), F32),
        jax.ShapeDtypeStruct((B, MEM_WIDTH, N_MEM), F32),
    )
    scratch = [
        pltpu.VMEM((2, N_CHUNKS, tm, PROJ_COLS), F32),
        pltpu.VMEM((T // TK, SB_WIDTH, TK), BF16),
        pltpu.VMEM((T // TK, SB_WIDTH, TK), BF16),
        pltpu.VMEM((MEM_WIDTH, N_MEM), BF16),
        pltpu.VMEM((MEM_WIDTH, N_MEM), BF16),
        pltpu.VMEM((tm // TQ, PAIRS, 2 * TQ, LANES), BF16),
        pltpu.VMEM((tm // TQ, PAIRS, 2 * TQ, LANES), F32),
        pltpu.VMEM((tm // TQ, PAIRS, 2 * TQ, LANES), F32),
        pltpu.VMEM((HALO + tm, CONV_WIDTH), F32),
        pltpu.VMEM((tm, D_MODEL), BF16),
        pltpu.VMEM((2 * SB_WIDTH, D_MODEL), BF16),
    ]
    return pl.pallas_call(
        functools.partial(_prompt_kernel, tm=tm, alpha=alpha),
        grid=grid,
        in_specs=[
            row_spec(D_MODEL),
            pl.BlockSpec((1, tm, D_MODEL), lambda b, t: (b, jnp.minimum(t + 1, T // tm - 1), 0)),
            per_b(N_MEM, D_MODEL),
            _const_spec((D_MODEL, SB_WIDTH)),
            _const_spec((D_MODEL, H_WIDTH - SB_WIDTH)),
            _const_spec((D_MODEL, 2 * SB_WIDTH)),
            _const_spec((2 * MEM_WIDTH, D_MODEL)),
            _const_spec((D_MODEL, D_MODEL)),
            _const_spec((TK, 2 * TK)),
            _const_spec((CONV_K, CONV_WIDTH)),
            _const_spec((1, CONV_WIDTH)),
            _const_spec((1, CONV_WIDTH)),
            _const_spec((1, CONV_WIDTH)),
            _const_spec((1, D_MODEL)),
            _const_spec((1, D_MODEL)),
        ],
        out_specs=(
            row_spec(D_MODEL), col_spec(SB_WIDTH), col_spec(SB_WIDTH),
            per_b(HALO, CONV_WIDTH), per_b(MEM_WIDTH, N_MEM), per_b(MEM_WIDTH, N_MEM),
        ),
        out_shape=out_shape,
        scratch_shapes=scratch,
        compiler_params=pltpu.CompilerParams(
            dimension_semantics=("arbitrary", "arbitrary"),
            vmem_limit_bytes=VMEM_LIMIT),
        name="prompt_layer",
    )(x, x, mem, w_q, w_rest, w_kv, w_memt, w_out, u2, cw, cb, cg, cbeta, lg, lb)


def _sample_kernel(x_ref, ktwin_ref, vtwin_ref, ktc_hbm, vtc_hbm, cpast_ref, mktc_ref, mvtc_ref,
                   w_q_ref, w_rest_ref, w_kv_ref, w_out_ref, u2_ref, cw_ref, cb_ref, cg_ref, cbeta_ref,
                   lg_ref, lb_ref,
                   y_ref, k_ref, v_ref, conv_ref,
                   h_scr, ktnew_scr, vtnew_scr, ktbuf, vtbuf, sem, qm_scr, acc_scr, carry_scr,
                   upad_scr, mix_scr, *, group, n_new, win, past, alpha):
    step = pl.program_id(0)
    row, col = _stacked_index(n_new)
    n_win = win // TK
    n_blocks = past // TK

    xb = x_ref[...].astype(BF16)
    for c in range(IN_WIDTH // PROJ_COLS):
        _project(xb, (w_q_ref, w_rest_ref, w_kv_ref), h_scr, c)
    for c in range(SB_WIDTH // PROJ_COLS):
        cols = slice(c * PROJ_COLS, (c + 1) * PROJ_COLS)
        k_ref[:, cols] = h_scr[COL_K // PROJ_COLS + c]
        v_ref[:, cols] = h_scr[COL_V // PROJ_COLS + c]
    kvt = lax.dot_general(w_kv_ref[...], xb, TN_NT_DIMS, preferred_element_type=F32)
    ktnew_scr[...] = kvt[:SB_WIDTH].astype(BF16)
    vtnew_scr[...] = kvt[SB_WIDTH:].astype(BF16)

    for g in range(group):
        _build_qm(h_scr, slice(g * n_new, (g + 1) * n_new), qm_scr.at[g])
    acc_scr[...] = jnp.zeros(acc_scr.shape, F32)
    carry_scr[...] = jnp.zeros(carry_scr.shape, F32)
    carries = [carry_scr.at[g] for g in range(group)]

    def part(g, get_kt, get_vt, mask):
        return _whole_part(qm_scr.at[g], acc_scr.at[g], carry_scr.at[g], get_kt, get_vt, mask)

    def own_past(g):
        rel = col - g * n_new
        return jnp.logical_and(rel >= 0, rel < row)

    _sb_blocks([part(g, lambda p: ktnew_scr[p * LANES:(p + 1) * LANES, :],
                     lambda p: vtnew_scr[p * LANES:(p + 1) * LANES, :], own_past(g))
                for g in range(group)], u2_ref)

    def win_blocks(j):
        cols = slice(win - (j + 1) * TK, win - j * TK)
        _sb_blocks([part(g, lambda p, g=g: ktwin_ref[g, p * LANES:(p + 1) * LANES, cols].astype(BF16),
                         lambda p, g=g: vtwin_ref[g, p * LANES:(p + 1) * LANES, cols].astype(BF16),
                         None)
                    for g in range(group)], u2_ref)
        return _carry_max(carries)

    m = jnp.float32(0.0)
    for j in range(n_win):
        m = lax.cond(m > SKIP_LOG2, functools.partial(win_blocks, j), lambda m=m: m)

    def far_walk(g, _):
        stream = step * group + g

        def far_block(j):
            start = pl.multiple_of(past - (j + 1) * TK, TK)
            ck = pltpu.make_async_copy(ktc_hbm.at[stream, :, pl.ds(start, TK)], ktbuf, sem.at[0])
            cv = pltpu.make_async_copy(vtc_hbm.at[stream, :, pl.ds(start, TK)], vtbuf, sem.at[1])
            ck.start()
            cv.start()
            ck.wait()
            cv.wait()
            _sb_blocks([part(g, lambda p: ktbuf[p * LANES:(p + 1) * LANES, :].astype(BF16),
                             lambda p: vtbuf[p * LANES:(p + 1) * LANES, :].astype(BF16), None)],
                       u2_ref)

        own = [carry_scr.at[g]]
        _sb_walk(jnp.int32(n_win), n_blocks, _carry_max(own), far_block, own)
        return 0

    lax.fori_loop(0, group, far_walk, 0)
    for g in range(group):
        rows = slice(g * n_new, (g + 1) * n_new)
        _sb_finish(h_scr, rows, acc_scr.at[g], mix_scr)
        _mem_attend(h_scr, rows, n_new, mktc_ref[g].astype(BF16), mvtc_ref[g].astype(BF16), mix_scr)

    u = (_hcols(h_scr, slice(0, group * n_new), COL_CA, CONV_WIDTH)
         * _sigmoid(_hcols(h_scr, slice(0, group * n_new), COL_CB, CONV_WIDTH)))
    for g in range(group):
        base = g * (HALO + n_new)
        upad_scr[base:base + HALO, :] = cpast_ref[g]
        upad_scr[base + HALO:base + HALO + n_new, :] = u[g * n_new:(g + 1) * n_new]
        _conv_rows(h_scr, g * n_new, upad_scr, base, n_new, mix_scr, cw_ref, cb_ref, cg_ref, cbeta_ref)
        conv_ref[g] = upad_scr[base + n_new:base + n_new + HALO, :]

    y_ref[...] = _out_norm(x_ref[...], mix_scr, w_out_ref, lg_ref, lb_ref, alpha)


def _sample_call(x, ktc, vtc, cpast, mktc, mvtc, w_q, w_rest, w_kv, w_out, u2, cw, cb, cg, cbeta, lg, lb,
                 alpha, win):
    S, n_new, _ = x.shape
    past = ktc.shape[2]
    group = TK // n_new
    assert group * n_new == TK and S % group == 0 and past % win == 0 and win % TK == 0
    assert n_new % 16 == 0 and n_new >= CONV_STATE
    rows = group * n_new
    x2 = x.reshape(S * n_new, D_MODEL)
    row_spec = lambda width: pl.BlockSpec((rows, width), lambda i: (i, 0))
    grp = lambda r, width: pl.BlockSpec((group, r, width), lambda i: (i, 0, 0))
    win_spec = pl.BlockSpec((group, SB_WIDTH, win), lambda i: (i, 0, past // win - 1))
    any_spec = pl.BlockSpec(memory_space=pl.ANY)
    out_shape = (
        jax.ShapeDtypeStruct((S * n_new, D_MODEL), F32),
        jax.ShapeDtypeStruct((S * n_new, SB_WIDTH), F32),
        jax.ShapeDtypeStruct((S * n_new, SB_WIDTH), F32),
        jax.ShapeDtypeStruct((S, HALO, CONV_WIDTH), F32),
    )
    scratch = [
        pltpu.VMEM((IN_WIDTH // PROJ_COLS, rows, PROJ_COLS), F32),
        pltpu.VMEM((SB_WIDTH, TK), BF16),
        pltpu.VMEM((SB_WIDTH, TK), BF16),
        pltpu.VMEM((SB_WIDTH, TK), F32),
        pltpu.VMEM((SB_WIDTH, TK), F32),
        pltpu.SemaphoreType.DMA((2,)),
        pltpu.VMEM((group, PAIRS, 2 * n_new, LANES), BF16),
        pltpu.VMEM((group, PAIRS, 2 * n_new, LANES), F32),
        pltpu.VMEM((group, PAIRS, 2 * n_new, LANES), F32),
        pltpu.VMEM((group * (HALO + n_new), CONV_WIDTH), F32),
        pltpu.VMEM((rows, D_MODEL), BF16),
    ]
    return pl.pallas_call(
        functools.partial(_sample_kernel, group=group, n_new=n_new, win=win, past=past, alpha=alpha),
        grid=(S // group,),
        in_specs=[
            row_spec(D_MODEL), win_spec, win_spec, any_spec, any_spec,
            grp(HALO, CONV_WIDTH), grp(MEM_WIDTH, N_MEM), grp(MEM_WIDTH, N_MEM),
            _const_spec((D_MODEL, SB_WIDTH)),
            _const_spec((D_MODEL, H_WIDTH - SB_WIDTH)),
            _const_spec((D_MODEL, 2 * SB_WIDTH)),
            _const_spec((D_MODEL, D_MODEL)),
            _const_spec((TK, 2 * TK)),
            _const_spec((CONV_K, CONV_WIDTH)),
            _const_spec((1, CONV_WIDTH)),
            _const_spec((1, CONV_WIDTH)),
            _const_spec((1, CONV_WIDTH)),
            _const_spec((1, D_MODEL)),
            _const_spec((1, D_MODEL)),
        ],
        out_specs=(row_spec(D_MODEL), row_spec(SB_WIDTH), row_spec(SB_WIDTH), grp(HALO, CONV_WIDTH)),
        out_shape=out_shape,
        scratch_shapes=scratch,
        compiler_params=pltpu.CompilerParams(
            dimension_semantics=("arbitrary",),
            vmem_limit_bytes=VMEM_LIMIT),
        name="sample_layer",
    )(x2, ktc, vtc, ktc, vtc, cpast, mktc, mvtc, w_q, w_rest, w_kv, w_out, u2, cw, cb, cg, cbeta,
      lg, lb)


def _prefix_matrix():
    j = lax.broadcasted_iota(jnp.int32, (TK, TK), 0)
    s = lax.broadcasted_iota(jnp.int32, (TK, TK), 1)
    return jnp.concatenate([(j > s).astype(BF16), jnp.ones((TK, TK), BF16)], axis=1)


def _time_minor(a):
    n, time, heads, dim = a.shape
    return jnp.transpose(a, (0, 2, 3, 1)).reshape(n, heads * dim, time)


def _time_major(a, heads):
    n, width, time = a.shape
    return jnp.transpose(a.reshape(n, heads, width // heads, time), (0, 3, 1, 2))[None]


def kernel(x_prompt, x_sample, cache_sb_k, cache_sb_v, cache_conv, cache_mem_k, cache_mem_v, mem_prompt, w_in, w_mem_kv, conv_w, conv_b, conv_ln_g, conv_ln_b, w_out, ln_g, ln_b):
    depth = w_in.shape[0]
    assert depth == 1, "single layer only"
    alpha = (2 * depth) ** 0.25
    S, n_new, _ = x_sample.shape

    u2 = _prefix_matrix()
    w = w_in[0]
    w_q = w[:, :REF_K].astype(BF16)
    w_rest = w[:, REF_V_END:].astype(BF16)
    w_kv = w[:, REF_K:REF_V_END].astype(BF16)
    w_memt = w_mem_kv[0].T.astype(BF16)
    w_out_b = w_out[0].astype(BF16)
    cw = conv_w[0]
    cb = conv_b[0].reshape(1, CONV_WIDTH)
    cg = conv_ln_g[0].reshape(1, CONV_WIDTH)
    cbeta = conv_ln_b[0].reshape(1, CONV_WIDTH)
    lg = ln_g[0].reshape(1, D_MODEL)
    lb = ln_b[0].reshape(1, D_MODEL)

    yp, ktp, vtp, cp, mktp, mvtp = _prompt_call(
        x_prompt, mem_prompt, w_q, w_rest, w_kv, w_memt, w_out_b, u2, cw, cb, cg, cbeta, lg, lb,
        alpha, tm=PROMPT_TILE)

    cpast = jnp.pad(cache_conv[0], ((0, 0), (HALO_PAD, 0), (0, 0)))
    ys, ks, vs, cs = _sample_call(
        x_sample, _time_minor(cache_sb_k[0]), _time_minor(cache_sb_v[0]), cpast,
        _time_minor(cache_mem_k[0]), _time_minor(cache_mem_v[0]),
        w_q, w_rest, w_kv, w_out_b, u2, cw, cb, cg, cbeta, lg, lb, alpha, win=CACHE_WINDOW)

    return (
        yp,
        ys.reshape(S, n_new, D_MODEL),
        _time_major(ktp, SB_HEADS),
        _time_major(vtp, SB_HEADS),
        cp[None, :, HALO_PAD:, :],
        _time_major(mktp, MEM_HEADS),
        _time_major(mvtp, MEM_HEADS),
        ks.reshape(1, S, n_new, SB_HEADS, HEAD_DIM),
        vs.reshape(1, S, n_new, SB_HEADS, HEAD_DIM),
        cs[None, :, HALO_PAD:, :],
    )
```
